```python
import jax, jax.numpy as jnp
from jax import lax
import numpy as np

D_MODEL = 1024
BATCH = 2
SEQ = 16384
DEPTH = 2

MEM_LEN = 256
EPS = 1e-6
CONV_A_WIDTH = D_MODEL // 2
CONV_A_KERNEL = 31
DIL_PATTERNS = ((128, 1), (512, 4), (2048, 16))
DIL_GROUPS = len(DIL_PATTERNS)
DIL_HEADS = 4
DIL_HEAD_DIM = 64
DIL_WIDTH = DIL_GROUPS * DIL_HEADS * DIL_HEAD_DIM
DIL_OUT = DIL_HEADS * DIL_HEAD_DIM
BLOCK = 128
SC_WIDTH = D_MODEL // 2
SC_KERNEL = 3
N_BRANCH = 3
OFF_A = 0
OFF_Q = OFF_A + 2 * CONV_A_WIDTH
OFF_K = OFF_Q + DIL_WIDTH
OFF_V = OFF_K + DIL_WIDTH
OFF_CB = OFF_V + DIL_WIDTH
OFF_CC = OFF_CB + SC_WIDTH
OFF_CH = OFF_CC + SC_WIDTH
OFF_G = OFF_CH + SC_WIDTH
N_IN = OFF_G + N_BRANCH * D_MODEL
XATTN_HEADS = 4
XATTN_HEAD_DIM = D_MODEL // XATTN_HEADS
D_FF = ((8 * D_MODEL // 3 + 255) // 256) * 256
N_EXPERTS = 8
TOP_K = 2
MOE_FF = 7 * D_MODEL // 2
N_DENSE = (DEPTH + 1) // 2
N_MOE = DEPTH // 2

kernel_name = "hybrid_conv_dilated_shortconv_moe_block"


def rms_norm(x, g):
    x32 = x.astype(jnp.float32)
    y = x32 * lax.rsqrt(jnp.mean(x32 * x32, axis=-1, keepdims=True) + EPS)
    return (y * g.astype(jnp.float32)).astype(x.dtype)


def layer_norm(x, g, b):
    x32 = x.astype(jnp.float32)
    mu = jnp.mean(x32, axis=-1, keepdims=True)
    xc = x32 - mu
    y = xc * lax.rsqrt(jnp.mean(xc * xc, axis=-1, keepdims=True) + EPS)
    return (y * g.astype(jnp.float32) + b.astype(jnp.float32)).astype(x.dtype)


def causal_depthwise_conv(u, w):
    k_width, c = w.shape
    return lax.conv_general_dilated(
        u, w.astype(u.dtype)[:, None, :], window_strides=(1,), padding=[(k_width - 1, 0)],
        dimension_numbers=("NWC", "WIO", "NWC"), feature_group_count=c)


def dilated_window_attention(q, k, v, dilation, n_steps):
    b, s, h, dh = q.shape
    span = dilation * BLOCK
    s_pad = -(-s // span) * span
    nb = s_pad // span
    pad = [(0, 0), (0, s_pad - s), (0, 0), (0, 0)]

    def blocks(t):
        return jnp.pad(t, pad).reshape(b, nb, BLOCK, dilation, h, dh)

    def band(t):
        prev = jnp.pad(t[:, :-1], [(0, 0), (1, 0), (0, 0), (0, 0), (0, 0), (0, 0)])
        return jnp.concatenate([prev, t], axis=2)

    qb = blocks(q)
    kw, vw = band(blocks(k)), band(blocks(v))
    scores = jnp.einsum('bnqrhe,bnkrhe->bnrhqk', qb, kw) * (dh ** -0.5)
    qi = jnp.arange(BLOCK)[:, None]
    kj = jnp.arange(2 * BLOCK)[None, :]
    dist = BLOCK + qi - kj
    in_band = (dist >= 0) & (dist <= n_steps)
    has_prev = jnp.arange(nb)[:, None, None] > 0
    valid = in_band[None] & (has_prev | (kj >= BLOCK)[None])
    scores = jnp.where(valid[None, :, None, None], scores, -jnp.inf)
    m = jnp.max(scores, axis=-1, keepdims=True)
    p = jnp.exp(scores - m)
    den = jnp.sum(p, axis=-1, keepdims=True)
    out = jnp.einsum('bnrhqk,bnkrhe->bnqrhe', p / den, vw)
    lse = (m + jnp.log(den))[..., 0]
    out = out.reshape(b, s_pad, h, dh)[:, :s]
    lse = lse.transpose(0, 1, 4, 2, 3).reshape(b, s_pad, h)[:, :s]
    return out, lse


def hybrid_mixer(h, w_in, b_in, a_conv_w, a_conv_b, a_ln_g, a_ln_b, a_w_out,
                 b_w_out, c_conv_w, c_w_out, w_o):
    bsz, s, _ = h.shape
    z = h @ w_in + b_in
    a = z[..., OFF_A:OFF_A + CONV_A_WIDTH] * jax.nn.sigmoid(z[..., OFF_A + CONV_A_WIDTH:OFF_Q])
    a = causal_depthwise_conv(a, a_conv_w) + a_conv_b
    y_a = jax.nn.silu(layer_norm(a, a_ln_g, a_ln_b)) @ a_w_out
    def heads(t):
        return t.reshape(bsz, s, DIL_GROUPS, DIL_HEADS, DIL_HEAD_DIM).astype(jnp.float32)
    q, k, v = heads(z[..., OFF_Q:OFF_K]), heads(z[..., OFF_K:OFF_V]), heads(z[..., OFF_V:OFF_CB])
    outs, lses = [], []
    for g, (window, dilation) in enumerate(DIL_PATTERNS):
        o, l = dilated_window_attention(q[:, :, g], k[:, :, g], v[:, :, g], dilation, window // dilation)
        outs.append(o)
        lses.append(l)
    wts = jax.nn.softmax(jnp.stack(lses, axis=0), axis=0)
    o = jnp.sum(wts[..., None] * jnp.stack(outs, axis=0), axis=0)
    y_b = o.reshape(bsz, s, DIL_OUT).astype(h.dtype) @ b_w_out
    gb, gc, hc = z[..., OFF_CB:OFF_CC], z[..., OFF_CC:OFF_CH], z[..., OFF_CH:OFF_G]
    y_c = (gb * causal_depthwise_conv(gc * hc, c_conv_w)) @ c_w_out
    gates = jax.nn.sigmoid(z[..., OFF_G:].reshape(bsz, s, N_BRANCH, D_MODEL))
    merged = gates[..., 0, :] * y_a + gates[..., 1, :] * y_b + gates[..., 2, :] * y_c
    return merged @ w_o


def memory_cross_attention(h, mem_n, wq, wk, wv, wo):
    bsz, s, _ = h.shape
    q = (h @ wq).reshape(bsz, s, XATTN_HEADS, XATTN_HEAD_DIM).astype(jnp.float32)
    k = (mem_n @ wk).reshape(bsz, -1, XATTN_HEADS, XATTN_HEAD_DIM).astype(jnp.float32)
    v = (mem_n @ wv).reshape(bsz, -1, XATTN_HEADS, XATTN_HEAD_DIM).astype(jnp.float32)
    p = jax.nn.softmax(jnp.einsum('bshe,bmhe->bhsm', q, k) * (XATTN_HEAD_DIM ** -0.5), axis=-1)
    o = jnp.einsum('bhsm,bmhe->bshe', p, v).reshape(bsz, s, D_MODEL).astype(h.dtype)
    return o @ wo


def swiglu(h, w1, w3, w2):
    return (jax.nn.silu(h @ w1) * (h @ w3)) @ w2


def moe_swiglu(h, w_router, w1, w3, w2):
    logits = (h @ w_router).astype(jnp.float32)
    top_val, top_idx = lax.top_k(logits, TOP_K)
    top_w = jax.nn.softmax(top_val, axis=-1)
    combine = jnp.sum(jax.nn.one_hot(top_idx, N_EXPERTS, dtype=jnp.float32) * top_w[..., None], axis=-2)
    out = jnp.zeros_like(h)
    for e in range(N_EXPERTS):
        out = out + combine[..., e:e + 1].astype(h.dtype) * swiglu(h, w1[e], w3[e], w2[e])
    return out


def setup_inputs(seed: int = 0) -> dict:
    key = jax.random.key(seed)
    ks = iter(jax.random.split(key, 40))

    def nrm(shape, scale):
        return jax.random.normal(next(ks), shape, jnp.float32) * scale

    def gain(shape):
        return 1.0 + nrm(shape, 0.02)

    L = DEPTH
    return {
        "x": nrm((BATCH, SEQ, D_MODEL), 1.0),
        "mem": nrm((BATCH, MEM_LEN, D_MODEL), 1.0),
        "norm_mix_g": gain((L, D_MODEL)),
        "w_in": nrm((L, D_MODEL, N_IN), D_MODEL ** -0.5),
        "b_in": nrm((L, N_IN), 0.02),
        "a_conv_w": nrm((L, CONV_A_KERNEL, CONV_A_WIDTH), CONV_A_KERNEL ** -0.5),
        "a_conv_b": nrm((L, CONV_A_WIDTH), 0.02),
        "a_ln_g": gain((L, CONV_A_WIDTH)),
        "a_ln_b": nrm((L, CONV_A_WIDTH), 0.02),
        "a_w_out": nrm((L, CONV_A_WIDTH, D_MODEL), CONV_A_WIDTH ** -0.5),
        "b_w_out": nrm((L, DIL_OUT, D_MODEL), DIL_OUT ** -0.5),
        "c_conv_w": nrm((L, SC_KERNEL, SC_WIDTH), SC_KERNEL ** -0.5),
        "c_w_out": nrm((L, SC_WIDTH, D_MODEL), SC_WIDTH ** -0.5),
        "w_o": nrm((L, D_MODEL, D_MODEL), D_MODEL ** -0.5),
        "norm_x_g": gain((L, D_MODEL)),
        "norm_mem_g": gain((L, D_MODEL)),
        "xq_w": nrm((L, D_MODEL, D_MODEL), D_MODEL ** -0.5),
        "xk_w": nrm((L, D_MODEL, D_MODEL), D_MODEL ** -0.5),
        "xv_w": nrm((L, D_MODEL, D_MODEL), D_MODEL ** -0.5),
        "xo_w": nrm((L, D_MODEL, D_MODEL), D_MODEL ** -0.5),
        "norm_ffn_g": gain((L, D_MODEL)),
        "ffn_w1": nrm((N_DENSE, D_MODEL, D_FF), D_MODEL ** -0.5),
        "ffn_w3": nrm((N_DENSE, D_MODEL, D_FF), D_MODEL ** -0.5),
        "ffn_w2": nrm((N_DENSE, D_FF, D_MODEL), D_FF ** -0.5),
        "moe_router": nrm((N_MOE, D_MODEL, N_EXPERTS), D_MODEL ** -0.5),
        "moe_w1": nrm((N_MOE, N_EXPERTS, D_MODEL, MOE_FF), D_MODEL ** -0.5),
        "moe_w3": nrm((N_MOE, N_EXPERTS, D_MODEL, MOE_FF), D_MODEL ** -0.5),
        "moe_w2": nrm((N_MOE, N_EXPERTS, MOE_FF, D_MODEL), MOE_FF ** -0.5),
        "final_g": gain((D_MODEL,)),
    }


def reference(x, mem, norm_mix_g, w_in, b_in, a_conv_w, a_conv_b, a_ln_g, a_ln_b, a_w_out,
              b_w_out, c_conv_w, c_w_out, w_o, norm_x_g, norm_mem_g, xq_w, xk_w, xv_w, xo_w,
              norm_ffn_g, ffn_w1, ffn_w3, ffn_w2, moe_router, moe_w1, moe_w3, moe_w2, final_g):
    for layer in range(DEPTH):
        h = rms_norm(x, norm_mix_g[layer])
        x = x + hybrid_mixer(h, w_in[layer], b_in[layer], a_conv_w[layer], a_conv_b[layer],
                             a_ln_g[layer], a_ln_b[layer], a_w_out[layer], b_w_out[layer],
                             c_conv_w[layer], c_w_out[layer], w_o[layer])
        h = rms_norm(x, norm_x_g[layer])
        mem_n = rms_norm(mem, norm_mem_g[layer])
        x = x + memory_cross_attention(h, mem_n, xq_w[layer], xk_w[layer], xv_w[layer], xo_w[layer])
        h = rms_norm(x, norm_ffn_g[layer])
        if layer % 2 == 0:
            i = layer // 2
            x = x + swiglu(h, ffn_w1[i], ffn_w3[i], ffn_w2[i])
        else:
            i = layer // 2
            x = x + moe_swiglu(h, moe_router[i], moe_w1[i], moe_w3[i], moe_w2[i])
    return rms_norm(x, final_g)
```

```python
import functools

import jax
import jax.numpy as jnp
from jax import lax
from jax.experimental import pallas as pl
from jax.experimental.pallas import tpu as pltpu

D_MODEL = 1024
EPS = 1e-6
CONV_A_WIDTH = 512
CONV_A_KERNEL = 31
DIL_DILATIONS = (1, 4, 16)
DIL_HEADS = 4
DIL_HEAD_DIM = 64
DIL_GROUP_WIDTH = DIL_HEADS * DIL_HEAD_DIM
DIL_WIDTH = len(DIL_DILATIONS) * DIL_GROUP_WIDTH
ATT_BLOCK = 128
SC_WIDTH = 512
SC_KERNEL = 3
OFF_A = 0
OFF_Q = OFF_A + 2 * CONV_A_WIDTH
OFF_CB = OFF_Q + 3 * DIL_WIDTH
OFF_G = OFF_CB + 3 * SC_WIDTH
N_IN = OFF_G + 3 * D_MODEL
XATTN_HEADS = 4
XATTN_HEAD_DIM = D_MODEL // XATTN_HEADS
N_EXPERTS = 8
LANES = 128
NEG = -1e30

VMEM_LIMIT = 56 * 1024 * 1024
SEQ_TILE = 512

BF16 = jnp.bfloat16
F32 = jnp.float32


def _dot(a, b):
    return jnp.dot(a, b, preferred_element_type=F32)


def _dot_nt(a, b):
    return lax.dot_general(a, b, (((1,), (1,)), ((), ())), preferred_element_type=F32)


def _rms(x, g):
    ms = jnp.mean(x * x, axis=-1, keepdims=True)
    return x * lax.rsqrt(ms + EPS) * g


def _sigmoid(x):
    return 1.0 / (1.0 + jnp.exp(-x))


def _resident(shape):
    nd = len(shape)
    return pl.BlockSpec(shape, lambda *_: (0,) * nd, pipeline_mode=pl.Buffered(1))


def _params(sem):
    return pltpu.CompilerParams(dimension_semantics=sem, vmem_limit_bytes=VMEM_LIMIT)


def _mem_kv_kernel(mem_ref, g_ref, wk_ref, wv_ref, k_ref, v_ref):
    h = _rms(mem_ref[0], g_ref[...]).astype(BF16)
    k_ref[0] = _dot(h, wk_ref[...]).astype(BF16)
    v_ref[0] = _dot(h, wv_ref[...]).astype(BF16)


def _mem_kv(mem, g, wk, wv):
    b, m, d = mem.shape
    out = jax.ShapeDtypeStruct((b, m, d), BF16)
    return pl.pallas_call(
        _mem_kv_kernel,
        grid=(b,),
        in_specs=[pl.BlockSpec((1, m, d), lambda i: (i, 0, 0)), _resident((1, d)),
                  _resident((d, d)), _resident((d, d))],
        out_specs=[pl.BlockSpec((1, m, d), lambda i: (i, 0, 0))] * 2,
        out_shape=[out, out],
        compiler_params=_params(("arbitrary",)),
        name="mem_kv",
    )(mem, g, wk, wv)


A_HALO = 32
C_HALO = 8
CONV_ROWS = 16


def _mixer_in_kernel(x_ref, g_ref, wa_ref, ba_ref, wqkv_ref, bqkv_ref, wc_ref, bc_ref, wg_ref, bg_ref,
                     acw_ref, acb_ref, alg_ref, alb_ref, awo_ref, ccw_ref, cwo_ref,
                     *out_and_scratch):
    qkv_refs = out_and_scratch[:9]
    part_ref, g1_ref = out_and_scratch[9:11]
    abuf, act, cbuf = out_and_scratch[11:]
    t = x_ref.shape[1]

    @pl.when(pl.program_id(1) == 0)
    def _():
        abuf[0:A_HALO, :] = jnp.zeros((A_HALO, CONV_A_WIDTH), F32)
        cbuf[0:C_HALO, :] = jnp.zeros((C_HALO, SC_WIDTH), F32)

    h = _rms(x_ref[0], g_ref[...]).astype(BF16)

    for i, ref in enumerate(qkv_refs):
        c0 = i * DIL_GROUP_WIDTH
        z = _dot(h, wqkv_ref[:, c0:c0 + DIL_GROUP_WIDTH]) + bqkv_ref[:, c0:c0 + DIL_GROUP_WIDTH]
        if i < 3:
            z = z * (DIL_HEAD_DIM ** -0.5)
        ref[0] = z.astype(BF16)

    za = _dot(h, wa_ref[...]) + ba_ref[...]
    abuf[A_HALO:A_HALO + t, :] = za[:, :CONV_A_WIDTH] * _sigmoid(za[:, CONV_A_WIDTH:])

    def conv_chunk(i, carry):
        base = pl.multiple_of(i * CONV_ROWS, CONV_ROWS)
        win = abuf[pl.ds(base, CONV_ROWS + A_HALO), :]
        acc = jnp.zeros((CONV_ROWS, CONV_A_WIDTH), F32)
        for k in range(CONV_A_KERNEL):
            off = A_HALO - (CONV_A_KERNEL - 1) + k
            acc = acc + win[off:off + CONV_ROWS, :] * acw_ref[k:k + 1, :]
        acc = acc + acb_ref[...]
        mu = jnp.mean(acc, axis=-1, keepdims=True)
        xc = acc - mu
        y = xc * lax.rsqrt(jnp.mean(xc * xc, axis=-1, keepdims=True) + EPS)
        y = y * alg_ref[...] + alb_ref[...]
        act[pl.ds(base, CONV_ROWS), :] = (y * _sigmoid(y)).astype(BF16)
        return carry

    lax.fori_loop(0, t // CONV_ROWS, conv_chunk, 0)
    abuf[0:A_HALO, :] = abuf[t:t + A_HALO, :]
    y_a = _dot(act[...], awo_ref[...])

    zc = _dot(h, wc_ref[...]) + bc_ref[...]
    cbuf[C_HALO:C_HALO + t, :] = zc[:, SC_WIDTH:2 * SC_WIDTH] * zc[:, 2 * SC_WIDTH:]
    conv_c = jnp.zeros((t, SC_WIDTH), F32)
    for k in range(SC_KERNEL):
        off = C_HALO - (SC_KERNEL - 1) + k
        conv_c = conv_c + cbuf[off:off + t, :] * ccw_ref[k:k + 1, :]
    cbuf[0:C_HALO, :] = cbuf[t:t + C_HALO, :]
    y_c = _dot((zc[:, :SC_WIDTH] * conv_c).astype(BF16), cwo_ref[...])

    zg = _dot(h, wg_ref[...]) + bg_ref[...]
    part = _sigmoid(zg[:, :D_MODEL]) * y_a + _sigmoid(zg[:, 2 * D_MODEL:]) * y_c
    part_ref[0] = part.astype(BF16)
    g1_ref[0] = _sigmoid(zg[:, D_MODEL:2 * D_MODEL]).astype(BF16)


def _mixer_in(x, g, wa, ba, wqkv, bqkv, wc, bc, wg, bg, acw, acb, alg, alb, awo, ccw, cwo):
    b, s, d = x.shape
    t = SEQ_TILE
    tile = lambda w: pl.BlockSpec((1, t, w), lambda i, j: (i, j, 0))
    weights = (g, wa, ba, wqkv, bqkv, wc, bc, wg, bg, acw, acb, alg, alb, awo, ccw, cwo)
    qkv_shape = jax.ShapeDtypeStruct((b, s, DIL_GROUP_WIDTH), BF16)
    wide = jax.ShapeDtypeStruct((b, s, d), BF16)
    return pl.pallas_call(
        _mixer_in_kernel,
        grid=(b, s // t),
        in_specs=[tile(d)] + [_resident(w.shape) for w in weights],
        out_specs=[tile(DIL_GROUP_WIDTH)] * 9 + [tile(d)] * 2,
        out_shape=[qkv_shape] * 9 + [wide] * 2,
        scratch_shapes=[pltpu.VMEM((t + A_HALO, CONV_A_WIDTH), F32),
                        pltpu.VMEM((t, CONV_A_WIDTH), BF16),
                        pltpu.VMEM((t + C_HALO, SC_WIDTH), F32)],
        compiler_params=_params(("arbitrary", "arbitrary")),
        name="mixer_in",
    )(x, *weights)


ATT_SUB = 4


def _dil_attn_kernel(q_ref, kc_ref, kp_ref, vc_ref, vp_ref, o_ref, l_ref):
    first = pl.program_id(1) == 0
    qi = lax.broadcasted_iota(jnp.int32, (ATT_BLOCK, ATT_BLOCK), 0)
    kj = lax.broadcasted_iota(jnp.int32, (ATT_BLOCK, ATT_BLOCK), 1)
    bias_prev = jnp.where(kj >= qi, 0.0, NEG).astype(F32)
    bias_cur = jnp.where(kj <= qi, 0.0, NEG).astype(F32)
    lane = lax.broadcasted_iota(jnp.int32, (ATT_BLOCK, DIL_GROUP_WIDTH), 1)
    head_of_lane = lane // DIL_HEAD_DIM

    for sb in range(ATT_SUB):
        rows = slice(sb * ATT_BLOCK, (sb + 1) * ATT_BLOCK)
        q = q_ref[0, rows, :]
        if sb == 0:
            k_prev, v_prev = kp_ref[0], vp_ref[0]
            b_prev = jnp.where(first, NEG, bias_prev)
        else:
            prev_rows = slice((sb - 1) * ATT_BLOCK, sb * ATT_BLOCK)
            k_prev, v_prev = kc_ref[0, prev_rows, :], vc_ref[0, prev_rows, :]
            b_prev = bias_prev
        k_cur, v_cur = kc_ref[0, rows, :], vc_ref[0, rows, :]
        out = jnp.zeros((ATT_BLOCK, DIL_GROUP_WIDTH), F32)
        lse = jnp.zeros((ATT_BLOCK, DIL_GROUP_WIDTH), F32)
        for hd in range(DIL_HEADS):
            in_head = head_of_lane == hd
            qh = jnp.where(in_head, q, jnp.zeros_like(q))
            s_prev = _dot_nt(qh, k_prev) + b_prev
            s_cur = _dot_nt(qh, k_cur) + bias_cur
            m = jnp.maximum(jnp.max(s_prev, axis=-1, keepdims=True),
                            jnp.max(s_cur, axis=-1, keepdims=True))
            p_prev = jnp.exp(s_prev - m)
            p_cur = jnp.exp(s_cur - m)
            den = jnp.sum(p_prev, axis=-1, keepdims=True) + jnp.sum(p_cur, axis=-1, keepdims=True)
            inv = 1.0 / den
            vh_prev = jnp.where(in_head, v_prev, jnp.zeros_like(v_prev))
            vh_cur = jnp.where(in_head, v_cur, jnp.zeros_like(v_cur))
            out = out + _dot((p_prev * inv).astype(BF16), vh_prev) + _dot((p_cur * inv).astype(BF16), vh_cur)
            lse = jnp.where(in_head, m + jnp.log(den), lse)
        o_ref[0, rows, :] = out.astype(BF16)
        l_ref[0, rows, :] = lse


def _dil_attn(q, k, v, dilation):
    b, s, w = q.shape
    rows = s // dilation
    q, k, v = (a.reshape(b, rows, dilation * w) for a in (q, k, v))
    lq = ATT_SUB * ATT_BLOCK
    cur = pl.BlockSpec((1, lq, w), lambda i, n, r: (i, n, r))
    prev = pl.BlockSpec((1, ATT_BLOCK, w), lambda i, n, r: (i, jnp.maximum(n * ATT_SUB - 1, 0), r))
    o, l = pl.pallas_call(
        _dil_attn_kernel,
        grid=(b, rows // lq, dilation),
        in_specs=[cur, cur, prev, cur, prev],
        out_specs=[cur, cur],
        out_shape=[jax.ShapeDtypeStruct((b, rows, dilation * w), BF16),
                   jax.ShapeDtypeStruct((b, rows, dilation * w), F32)],
        compiler_params=_params(("arbitrary", "arbitrary", "arbitrary")),
        name=f"dil_attn_d{dilation}",
    )(q, k, k, v, v)
    return o.reshape(b, s, w), l.reshape(b, s, w)


def _post_kernel(x_ref, part_ref, g1_ref, o0_ref, o1_ref, o2_ref, l0_ref, l1_ref, l2_ref,
                 bwo_ref, wo_ref, gx_ref, xq_ref, km_ref, vm_ref, xo_ref, out_ref):
    l0, l1, l2 = l0_ref[0], l1_ref[0], l2_ref[0]
    top = jnp.maximum(jnp.maximum(l0, l1), l2)
    e0, e1, e2 = jnp.exp(l0 - top), jnp.exp(l1 - top), jnp.exp(l2 - top)
    o = (e0 * o0_ref[0].astype(F32) + e1 * o1_ref[0].astype(F32) + e2 * o2_ref[0].astype(F32)) / (e0 + e1 + e2)
    y_b = _dot(o.astype(BF16), bwo_ref[...])
    merged = part_ref[0].astype(F32) + g1_ref[0].astype(F32) * y_b
    x = x_ref[0] + _dot(merged.astype(BF16), wo_ref[...])

    h = _rms(x, gx_ref[...]).astype(BF16)
    q = (_dot(h, xq_ref[...]) * (XATTN_HEAD_DIM ** -0.5)).astype(BF16)
    for hd in range(XATTN_HEADS):
        cols = slice(hd * XATTN_HEAD_DIM, (hd + 1) * XATTN_HEAD_DIM)
        s = _dot_nt(q[:, cols], km_ref[0, :, cols])
        m = jnp.max(s, axis=-1, keepdims=True)
        p = jnp.exp(s - m)
        p = p * (1.0 / jnp.sum(p, axis=-1, keepdims=True))
        oh = _dot(p.astype(BF16), vm_ref[0, :, cols])
        x = x + _dot(oh.astype(BF16), xo_ref[cols, :])
    out_ref[0] = x


def _post(x, part, g1, outs, lses, bwo, wo, gx, xq, km, vm, xo):
    b, s, d = x.shape
    t = SEQ_TILE
    tile = lambda w: pl.BlockSpec((1, t, w), lambda i, j: (i, j, 0))
    mem_spec = pl.BlockSpec((1,) + km.shape[1:], lambda i, j: (i, 0, 0))
    weights_a = (bwo, wo, gx, xq)
    return pl.pallas_call(
        _post_kernel,
        grid=(b, s // t),
        in_specs=[tile(d)] * 3 + [tile(DIL_GROUP_WIDTH)] * 6 + [_resident(w.shape) for w in weights_a]
                 + [mem_spec, mem_spec, _resident(xo.shape)],
        out_specs=tile(d),
        out_shape=jax.ShapeDtypeStruct((b, s, d), F32),
        compiler_params=_params(("arbitrary", "arbitrary")),
        name="post_mixer",
    )(x, part, g1, *outs, *lses, *weights_a, km, vm, xo)


FFN_CHUNKS = 2


def _ffn_kernel(xs_ref, acc_ref, g_ref, w1_ref, w3_ref, w2_ref, *rest, expert, final):
    rest = list(rest)
    cw_ref = rest.pop(0) if expert is not None else None
    fg_ref = rest.pop(0) if final else None
    out_ref, = rest
    w1_ref, w3_ref, w2_ref = (r if expert is None else r.at[0] for r in (w1_ref, w3_ref, w2_ref))
    h = _rms(xs_ref[...], g_ref[...]).astype(BF16)
    f = w1_ref.shape[1]
    fc = f // FFN_CHUNKS
    y = jnp.zeros(xs_ref.shape, F32)
    for c in range(FFN_CHUNKS):
        cols = slice(c * fc, (c + 1) * fc)
        a = _dot(h, w1_ref[:, cols])
        u = (a * _sigmoid(a) * _dot(h, w3_ref[:, cols])).astype(BF16)
        y = y + _dot(u, w2_ref[cols, :])
    if expert is not None:
        y = y * cw_ref[:, expert:expert + 1]
    out = acc_ref[...] + y
    if final:
        out = _rms(out, fg_ref[...])
    out_ref[...] = out


def _ffn(xs, acc, g, w1, w3, w2, combine=None, expert=None, final_g=None):
    n, d = xs.shape
    t = SEQ_TILE
    row = lambda w: pl.BlockSpec((t, w), lambda i: (i, 0))
    if expert is None:
        wspecs = [_resident(w.shape) for w in (w1, w3, w2)]
    else:
        wspecs = [pl.BlockSpec((1,) + w.shape[1:], lambda i: (expert, 0, 0), pipeline_mode=pl.Buffered(1))
                  for w in (w1, w3, w2)]
    args = [xs, acc, g, w1, w3, w2]
    in_specs = [row(d), row(d), _resident(g.shape)] + wspecs
    if expert is not None:
        args.append(combine)
        in_specs.append(row(LANES))
    if final_g is not None:
        args.append(final_g)
        in_specs.append(_resident(final_g.shape))
    return pl.pallas_call(
        functools.partial(_ffn_kernel, expert=expert, final=final_g is not None),
        grid=(n // t,),
        in_specs=in_specs,
        out_specs=row(d),
        out_shape=jax.ShapeDtypeStruct((n, d), F32),
        compiler_params=_params(("arbitrary",)),
        name="ffn" if expert is None else f"ffn_expert{expert}",
    )(*args)


def _router_kernel(x_ref, g_ref, wr_ref, cw_ref):
    h = _rms(x_ref[...], g_ref[...]).astype(BF16)
    logits = _dot(h, wr_ref[...])
    lane = lax.broadcasted_iota(jnp.int32, logits.shape, 1)
    lg = jnp.where(lane < N_EXPERTS, logits, -jnp.inf)
    m1 = jnp.max(lg, axis=-1, keepdims=True)
    i1 = jnp.min(jnp.where(lg == m1, lane, LANES), axis=-1, keepdims=True)
    lg2 = jnp.where(lane == i1, -jnp.inf, lg)
    m2 = jnp.max(lg2, axis=-1, keepdims=True)
    i2 = jnp.min(jnp.where(lg2 == m2, lane, LANES), axis=-1, keepdims=True)
    e = jnp.exp(m2 - m1)
    w_top = 1.0 / (1.0 + e)
    cw_ref[...] = jnp.where(lane == i1, w_top, 0.0) + jnp.where(lane == i2, e * w_top, 0.0)


def _router(x, g, wr):
    n, d = x.shape
    t = SEQ_TILE
    return pl.pallas_call(
        _router_kernel,
        grid=(n // t,),
        in_specs=[pl.BlockSpec((t, d), lambda i: (i, 0)), _resident(g.shape), _resident(wr.shape)],
        out_specs=pl.BlockSpec((t, LANES), lambda i: (i, 0)),
        out_shape=jax.ShapeDtypeStruct((n, LANES), F32),
        compiler_params=_params(("arbitrary",)),
        name="router",
    )(x, g, wr)


def kernel(x, mem, norm_mix_g, w_in, b_in, a_conv_w, a_conv_b, a_ln_g, a_ln_b, a_w_out, b_w_out, c_conv_w, c_w_out, w_o, norm_x_g, norm_mem_g, xq_w, xk_w, xv_w, xo_w, norm_ffn_g, ffn_w1, ffn_w3, ffn_w2, moe_router, moe_w1, moe_w3, moe_w2, final_g):
    bsz, seq, d = x.shape
    depth = w_in.shape[0]
    row = lambda v: v.reshape(1, -1).astype(F32)
    bf = lambda w: w.astype(BF16)
    for layer in range(depth):
        wl, bl = w_in[layer], b_in[layer]
        cols = lambda lo, hi: (bf(wl[:, lo:hi]), row(bl[lo:hi]))
        wa, ba = cols(OFF_A, OFF_Q)
        wqkv, bqkv = cols(OFF_Q, OFF_CB)
        wc, bc = cols(OFF_CB, OFF_G)
        wg, bg = cols(OFF_G, N_IN)
        outs = _mixer_in(x, row(norm_mix_g[layer]), wa, ba, wqkv, bqkv, wc, bc, wg, bg,
                         a_conv_w[layer], row(a_conv_b[layer]), row(a_ln_g[layer]), row(a_ln_b[layer]),
                         bf(a_w_out[layer]), c_conv_w[layer], bf(c_w_out[layer]))
        qs, ks, vs = outs[0:3], outs[3:6], outs[6:9]
        part, g1 = outs[9:11]
        att = [_dil_attn(qs[i], ks[i], vs[i], dil) for i, dil in enumerate(DIL_DILATIONS)]
        km, vm = _mem_kv(mem, row(norm_mem_g[layer]), bf(xk_w[layer]), bf(xv_w[layer]))
        x = _post(x, part, g1, [a[0] for a in att], [a[1] for a in att], bf(b_w_out[layer]), bf(w_o[layer]),
                  row(norm_x_g[layer]), bf(xq_w[layer]), km, vm, bf(xo_w[layer]))
        last = layer == depth - 1
        fg = row(final_g) if last else None
        x2 = x.reshape(bsz * seq, d)
        gf = row(norm_ffn_g[layer])
        i = layer // 2
        if layer % 2 == 0:
            x2 = _ffn(x2, x2, gf, bf(ffn_w1[i]), bf(ffn_w3[i]), bf(ffn_w2[i]), final_g=fg)
        else:
            wr = jnp.zeros((d, LANES), F32).at[:, :N_EXPERTS].set(moe_router[i]).astype(BF16)
            combine = _router(x2, gf, wr)
            w1, w3, w2 = bf(moe_w1[i]), bf(moe_w3[i]), bf(moe_w2[i])
            acc = x2
            for e in range(N_EXPERTS):
                acc = _ffn(x2, acc, gf, w1, w3, w2, combine=combine, expert=e,
                           final_g=fg if e == N_EXPERTS - 1 else None)
            x2 = acc
        x = x2.reshape(bsz, seq, d)
    return x
```

```python
import functools

import jax
import jax.numpy as jnp
from jax import lax
from jax.experimental import pallas as pl
from jax.experimental.pallas import tpu as pltpu

D_MODEL = 1024
EPS = 1e-6
CONV_A_WIDTH = 512
CONV_A_KERNEL = 31
DIL_DILATIONS = (1, 4, 16)
DIL_HEADS = 4
DIL_HEAD_DIM = 64
DIL_GROUP_WIDTH = DIL_HEADS * DIL_HEAD_DIM
DIL_WIDTH = len(DIL_DILATIONS) * DIL_GROUP_WIDTH
ATT_BLOCK = 128
SC_WIDTH = 512
SC_KERNEL = 3
OFF_A = 0
OFF_Q = OFF_A + 2 * CONV_A_WIDTH
OFF_CB = OFF_Q + 3 * DIL_WIDTH
OFF_G = OFF_CB + 3 * SC_WIDTH
N_IN = OFF_G + 3 * D_MODEL
XATTN_HEADS = 4
XATTN_HEAD_DIM = D_MODEL // XATTN_HEADS
N_EXPERTS = 8
LANES = 128
NEG = -1e30

VMEM_LIMIT = 56 * 1024 * 1024
SEQ_TILE = 512

BF16 = jnp.bfloat16
F32 = jnp.float32


def _dot(a, b):
    return jnp.dot(a, b, preferred_element_type=F32)


def _dot_nt(a, b):
    return lax.dot_general(a, b, (((1,), (1,)), ((), ())), preferred_element_type=F32)


def _rms(x, g):
    ms = jnp.mean(x * x, axis=-1, keepdims=True)
    return x * lax.rsqrt(ms + EPS) * g


def _sigmoid(x):
    return 1.0 / (1.0 + jnp.exp(-x))


def _resident(shape):
    nd = len(shape)
    return pl.BlockSpec(shape, lambda *_: (0,) * nd, pipeline_mode=pl.Buffered(1))


def _params(sem):
    return pltpu.CompilerParams(dimension_semantics=sem, vmem_limit_bytes=VMEM_LIMIT)


def _mem_kv_kernel(mem_ref, g_ref, wk_ref, wv_ref, k_ref, v_ref):
    h = _rms(mem_ref[0], g_ref[...]).astype(BF16)
    k_ref[0] = _dot(h, wk_ref[...]).astype(BF16)
    v_ref[0] = _dot(h, wv_ref[...]).astype(BF16)


def _mem_kv(mem, g, wk, wv):
    b, m, d = mem.shape
    out = jax.ShapeDtypeStruct((b, m, d), BF16)
    return pl.pallas_call(
        _mem_kv_kernel,
        grid=(b,),
        in_specs=[pl.BlockSpec((1, m, d), lambda i: (i, 0, 0)), _resident((1, d)),
                  _resident((d, d)), _resident((d, d))],
        out_specs=[pl.BlockSpec((1, m, d), lambda i: (i, 0, 0))] * 2,
        out_shape=[out, out],
        compiler_params=_params(("arbitrary",)),
        name="mem_kv",
    )(mem, g, wk, wv)


A_HALO = 32
C_HALO = 8
CONV_ROWS = 16


def _mixer_in_kernel(x_ref, g_ref, wa_ref, ba_ref, wqkv_ref, bqkv_ref, wc_ref, bc_ref, wg_ref, bg_ref,
                     acw_ref, acb_ref, alg_ref, alb_ref, awo_ref, ccw_ref, cwo_ref,
                     *out_and_scratch):
    qkv_refs = out_and_scratch[:9]
    part_ref, g1_ref = out_and_scratch[9:11]
    abuf, act, cbuf = out_and_scratch[11:]
    t = x_ref.shape[1]

    @pl.when(pl.program_id(1) == 0)
    def _():
        abuf[0:A_HALO, :] = jnp.zeros((A_HALO, CONV_A_WIDTH), F32)
        cbuf[0:C_HALO, :] = jnp.zeros((C_HALO, SC_WIDTH), F32)

    h = _rms(x_ref[0], g_ref[...]).astype(BF16)

    for i, ref in enumerate(qkv_refs):
        c0 = i * DIL_GROUP_WIDTH
        z = _dot(h, wqkv_ref[:, c0:c0 + DIL_GROUP_WIDTH]) + bqkv_ref[:, c0:c0 + DIL_GROUP_WIDTH]
        if i < 3:
            z = z * (DIL_HEAD_DIM ** -0.5)
        ref[0] = z.astype(BF16)

    za = _dot(h, wa_ref[...]) + ba_ref[...]
    abuf[A_HALO:A_HALO + t, :] = za[:, :CONV_A_WIDTH] * _sigmoid(za[:, CONV_A_WIDTH:])

    def conv_chunk(i, carry):
        base = pl.multiple_of(i * CONV_ROWS, CONV_ROWS)
        win = abuf[pl.ds(base, CONV_ROWS + A_HALO), :]
        acc = jnp.zeros((CONV_ROWS, CONV_A_WIDTH), F32)
        for k in range(CONV_A_KERNEL):
            off = A_HALO - (CONV_A_KERNEL - 1) + k
            acc = acc + win[off:off + CONV_ROWS, :] * acw_ref[k:k + 1, :]
        acc = acc + acb_ref[...]
        mu = jnp.mean(acc, axis=-1, keepdims=True)
        xc = acc - mu
        y = xc * lax.rsqrt(jnp.mean(xc * xc, axis=-1, keepdims=True) + EPS)
        y = y * alg_ref[...] + alb_ref[...]
        act[pl.ds(base, CONV_ROWS), :] = (y * _sigmoid(y)).astype(BF16)
        return carry

    lax.fori_loop(0, t // CONV_ROWS, conv_chunk, 0)
    abuf[0:A_HALO, :] = abuf[t:t + A_HALO, :]
    y_a = _dot(act[...], awo_ref[...])

    zc = _dot(h, wc_ref[...]) + bc_ref[...]
    cbuf[C_HALO:C_HALO + t, :] = zc[:, SC_WIDTH:2 * SC_WIDTH] * zc[:, 2 * SC_WIDTH:]
    conv_c = jnp.zeros((t, SC_WIDTH), F32)
    for k in range(SC_KERNEL):
        off = C_HALO - (SC_KERNEL - 1) + k
        conv_c = conv_c + cbuf[off:off + t, :] * ccw_ref[k:k + 1, :]
    cbuf[0:C_HALO, :] = cbuf[t:t + C_HALO, :]
    y_c = _dot((zc[:, :SC_WIDTH] * conv_c).astype(BF16), cwo_ref[...])

    zg = _dot(h, wg_ref[...]) + bg_ref[...]
    part = _sigmoid(zg[:, :D_MODEL]) * y_a + _sigmoid(zg[:, 2 * D_MODEL:]) * y_c
    part_ref[0] = part.astype(BF16)
    g1_ref[0] = _sigmoid(zg[:, D_MODEL:2 * D_MODEL]).astype(BF16)


def _mixer_in(x, g, wa, ba, wqkv, bqkv, wc, bc, wg, bg, acw, acb, alg, alb, awo, ccw, cwo):
    b, s, d = x.shape
    t = SEQ_TILE
    tile = lambda w: pl.BlockSpec((1, t, w), lambda i, j: (i, j, 0))
    weights = (g, wa, ba, wqkv, bqkv, wc, bc, wg, bg, acw, acb, alg, alb, awo, ccw, cwo)
    qkv_shape = jax.ShapeDtypeStruct((b, s, DIL_GROUP_WIDTH), BF16)
    wide = jax.ShapeDtypeStruct((b, s, d), BF16)
    return pl.pallas_call(
        _mixer_in_kernel,
        grid=(b, s // t),
        in_specs=[tile(d)] + [_resident(w.shape) for w in weights],
        out_specs=[tile(DIL_GROUP_WIDTH)] * 9 + [tile(d)] * 2,
        out_shape=[qkv_shape] * 9 + [wide] * 2,
        scratch_shapes=[pltpu.VMEM((t + A_HALO, CONV_A_WIDTH), F32),
                        pltpu.VMEM((t, CONV_A_WIDTH), BF16),
                        pltpu.VMEM((t + C_HALO, SC_WIDTH), F32)],
        compiler_params=_params(("arbitrary", "arbitrary")),
        name="mixer_in",
    )(x, *weights)


ATT_SUB = 4


def _dil_attn_kernel(q_ref, kc_ref, kp_ref, vc_ref, vp_ref, o_ref, l_ref):
    first = pl.program_id(1) == 0
    qi = lax.broadcasted_iota(jnp.int32, (ATT_BLOCK, ATT_BLOCK), 0)
    kj = lax.broadcasted_iota(jnp.int32, (ATT_BLOCK, ATT_BLOCK), 1)
    bias_prev = jnp.where(kj >= qi, 0.0, NEG).astype(F32)
    bias_cur = jnp.where(kj <= qi, 0.0, NEG).astype(F32)
    lane = lax.broadcasted_iota(jnp.int32, (ATT_BLOCK, DIL_GROUP_WIDTH), 1)
    head_of_lane = lane // DIL_HEAD_DIM

    for sb in range(ATT_SUB):
        rows = slice(sb * ATT_BLOCK, (sb + 1) * ATT_BLOCK)
        q = q_ref[0, rows, :]
        if sb == 0:
            k_prev, v_prev = kp_ref[0], vp_ref[0]
            b_prev = jnp.where(first, NEG, bias_prev)
        else:
            prev_rows = slice((sb - 1) * ATT_BLOCK, sb * ATT_BLOCK)
            k_prev, v_prev = kc_ref[0, prev_rows, :], vc_ref[0, prev_rows, :]
            b_prev = bias_prev
        k_cur, v_cur = kc_ref[0, rows, :], vc_ref[0, rows, :]
        out = jnp.zeros((ATT_BLOCK, DIL_GROUP_WIDTH), F32)
        lse = jnp.zeros((ATT_BLOCK, DIL_GROUP_WIDTH), F32)
        for hd in range(DIL_HEADS):
            in_head = head_of_lane == hd
            qh = jnp.where(in_head, q, jnp.zeros_like(q))
            s_prev = _dot_nt(qh, k_prev) + b_prev
            s_cur = _dot_nt(qh, k_cur) + bias_cur
            m = jnp.maximum(jnp.max(s_prev, axis=-1, keepdims=True),
                            jnp.max(s_cur, axis=-1, keepdims=True))
            p_prev = jnp.exp(s_prev - m)
            p_cur = jnp.exp(s_cur - m)
            den = jnp.sum(p_prev, axis=-1, keepdims=True) + jnp.sum(p_cur, axis=-1, keepdims=True)
            inv = 1.0 / den
            vh_prev = jnp.where(in_head, v_prev, jnp.zeros_like(v_prev))
            vh_cur = jnp.where(in_head, v_cur, jnp.zeros_like(v_cur))
            out = out + _dot((p_prev * inv).astype(BF16), vh_prev) + _dot((p_cur * inv).astype(BF16), vh_cur)
            lse = jnp.where(in_head, m + jnp.log(den), lse)
        o_ref[0, rows, :] = out.astype(BF16)
        l_ref[0, rows, :] = lse


def _dil_attn(q, k, v, dilation):
    b, s, w = q.shape
    rows = s // dilation
    q, k, v = (a.reshape(b, rows, dilation * w) for a in (q, k, v))
    lq = ATT_SUB * ATT_BLOCK
    cur = pl.BlockSpec((1, lq, w), lambda i, n, r: (i, n, r))
    prev = pl.BlockSpec((1, ATT_BLOCK, w), lambda i, n, r: (i, jnp.maximum(n * ATT_SUB - 1, 0), r))
    o, l = pl.pallas_call(
        _dil_attn_kernel,
        grid=(b, rows // lq, dilation),
        in_specs=[cur, cur, prev, cur, prev],
        out_specs=[cur, cur],
        out_shape=[jax.ShapeDtypeStruct((b, rows, dilation * w), BF16),
                   jax.ShapeDtypeStruct((b, rows, dilation * w), F32)],
        compiler_params=_params(("arbitrary", "arbitrary", "arbitrary")),
        name=f"dil_attn_d{dilation}",
    )(q, k, k, v, v)
    return o.reshape(b, s, w), l.reshape(b, s, w)


def _post_kernel(x_ref, part_ref, g1_ref, o0_ref, o1_ref, o2_ref, l0_ref, l1_ref, l2_ref,
                 bwo_ref, wo_ref, gx_ref, xq_ref, km_ref, vm_ref, xo_ref, out_ref):
    l0, l1, l2 = l0_ref[0], l1_ref[0], l2_ref[0]
    top = jnp.maximum(jnp.maximum(l0, l1), l2)
    e0, e1, e2 = jnp.exp(l0 - top), jnp.exp(l1 - top), jnp.exp(l2 - top)
    o = (e0 * o0_ref[0].astype(F32) + e1 * o1_ref[0].astype(F32) + e2 * o2_ref[0].astype(F32)) / (e0 + e1 + e2)
    y_b = _dot(o.astype(BF16), bwo_ref[...])
    merged = part_ref[0].astype(F32) + g1_ref[0].astype(F32) * y_b
    x = x_ref[0] + _dot(merged.astype(BF16), wo_ref[...])

    h = _rms(x, gx_ref[...]).astype(BF16)
    q = (_dot(h, xq_ref[...]) * (XATTN_HEAD_DIM ** -0.5)).astype(BF16)
    for hd in range(XATTN_HEADS):
        cols = slice(hd * XATTN_HEAD_DIM, (hd + 1) * XATTN_HEAD_DIM)
        s = _dot_nt(q[:, cols], km_ref[0, :, cols])
        m = jnp.max(s, axis=-1, keepdims=True)
        p = jnp.exp(s - m)
        p = p * (1.0 / jnp.sum(p, axis=-1, keepdims=True))
        oh = _dot(p.astype(BF16), vm_ref[0, :, cols])
        x = x + _dot(oh.astype(BF16), xo_ref[cols, :])
    out_ref[0] = x


def _post(x, part, g1, outs, lses, bwo, wo, gx, xq, km, vm, xo):
    b, s, d = x.shape
    t = SEQ_TILE
    tile = lambda w: pl.BlockSpec((1, t, w), lambda i, j: (i, j, 0))
    mem_spec = pl.BlockSpec((1,) + km.shape[1:], lambda i, j: (i, 0, 0))
    weights_a = (bwo, wo, gx, xq)
    return pl.pallas_call(
        _post_kernel,
        grid=(b, s // t),
        in_specs=[tile(d)] * 3 + [tile(DIL_GROUP_WIDTH)] * 6 + [_resident(w.shape) for w in weights_a]
                 + [mem_spec, mem_spec, _resident(xo.shape)],
        out_specs=tile(d),
        out_shape=jax.ShapeDtypeStruct((b, s, d), F32),
        compiler_params=_params(("arbitrary", "arbitrary")),
        name="post_mixer",
    )(x, part, g1, *outs, *lses, *weights_a, km, vm, xo)


FFN_CHUNKS = 2


def _ffn_kernel(x_ref, g_ref, w1_ref, w3_ref, w2_ref, *rest, final):
    fg_ref = rest[0] if final else None
    out_ref = rest[-1]
    x = x_ref[...]
    h = _rms(x, g_ref[...]).astype(BF16)
    fc = w1_ref.shape[1] // FFN_CHUNKS
    y = jnp.zeros(x.shape, F32)
    for c in range(FFN_CHUNKS):
        cols = slice(c * fc, (c + 1) * fc)
        a = _dot(h, w1_ref[:, cols])
        u = (a * _sigmoid(a) * _dot(h, w3_ref[:, cols])).astype(BF16)
        y = y + _dot(u, w2_ref[cols, :])
    out = x + y
    if final:
        out = _rms(out, fg_ref[...])
    out_ref[...] = out


def _ffn(x, g, w1, w3, w2, final_g=None):
    n, d = x.shape
    t = SEQ_TILE
    row = pl.BlockSpec((t, d), lambda i: (i, 0))
    args = [x, g, w1, w3, w2] + ([final_g] if final_g is not None else [])
    return pl.pallas_call(
        functools.partial(_ffn_kernel, final=final_g is not None),
        grid=(n // t,),
        in_specs=[row] + [_resident(a.shape) for a in args[1:]],
        out_specs=row,
        out_shape=jax.ShapeDtypeStruct((n, d), F32),
        compiler_params=_params(("arbitrary",)),
        name="ffn",
    )(*args)


MOE_TILE = 2048
MOE_SUB = 256
MOE_NSUB = MOE_TILE // MOE_SUB
MOE_ALIGN = 16
MOE_WIN = 128
MOE_WIN2 = MOE_ALIGN + MOE_SUB - MOE_WIN
MOE_BLOCK = 256
MOE_FC = 512
MOE_ROWS = MOE_TILE + MOE_BLOCK
META_ROWS = 16


def _router_kernel(x_ref, g_ref, wr_ref, h_ref, dest_ref, wt_ref, meta_ref, tri_ref):
    t = x_ref.shape[0]

    @pl.when(pl.program_id(0) == 0)
    def _():
        for rb in range(0, t, MOE_SUB):
            r = lax.broadcasted_iota(jnp.int32, (MOE_SUB, t), 0) + rb
            c = lax.broadcasted_iota(jnp.int32, (MOE_SUB, t), 1)
            tri_ref[rb:rb + MOE_SUB, :] = jnp.where(c < r, 1.0, 0.0).astype(BF16)

    h = _rms(x_ref[...], g_ref[...]).astype(BF16)
    h_ref[...] = h
    logits = _dot(h, wr_ref[...])
    lane = lax.broadcasted_iota(jnp.int32, logits.shape, 1)
    lg = jnp.where(lane < N_EXPERTS, logits, -jnp.inf)
    m1 = jnp.max(lg, axis=-1, keepdims=True)
    i1 = jnp.min(jnp.where(lg == m1, lane, LANES), axis=-1, keepdims=True)
    lg2 = jnp.where(lane == i1, -jnp.inf, lg)
    m2 = jnp.max(lg2, axis=-1, keepdims=True)
    i2 = jnp.min(jnp.where(lg2 == m2, lane, LANES), axis=-1, keepdims=True)
    e = jnp.exp(m2 - m1)
    w_top = 1.0 / (1.0 + e)
    cw = jnp.where(lane == i1, w_top, 0.0) + jnp.where(lane == i2, e * w_top, 0.0)
    ind = jnp.where(lane == i1, 1.0, 0.0) + jnp.where(lane == i2, 1.0, 0.0)
    rank = _dot(tri_ref[...], ind.astype(BF16))
    dest = jnp.where(ind > 0.0, rank, -1.0)
    dest_ref[0] = dest.T[:N_EXPERTS, :]
    wt_ref[0] = cw.T[:N_EXPERTS, :]
    rows = [rank[s * MOE_SUB:s * MOE_SUB + 1, :] for s in range(MOE_NSUB)]
    rows.append(rank[t - 1:t, :] + ind[t - 1:t, :])
    rows.append(jnp.zeros((META_ROWS - len(rows), LANES), F32))
    meta_ref[0] = jnp.concatenate(rows, axis=0)


def _router(x, g, wr):
    n, d = x.shape
    t = MOE_TILE
    nt = n // t
    per_tile = lambda r, c: pl.BlockSpec((1, r, c), lambda i: (i, 0, 0))
    return pl.pallas_call(
        _router_kernel,
        grid=(nt,),
        in_specs=[pl.BlockSpec((t, d), lambda i: (i, 0)), _resident(g.shape), _resident(wr.shape)],
        out_specs=[pl.BlockSpec((t, d), lambda i: (i, 0)), per_tile(N_EXPERTS, t), per_tile(N_EXPERTS, t),
                   per_tile(META_ROWS, LANES)],
        out_shape=[jax.ShapeDtypeStruct((n, d), BF16), jax.ShapeDtypeStruct((nt, N_EXPERTS, t), F32),
                   jax.ShapeDtypeStruct((nt, N_EXPERTS, t), F32), jax.ShapeDtypeStruct((nt, META_ROWS, LANES), F32)],
        scratch_shapes=[pltpu.VMEM((t, t), BF16)],
        compiler_params=_params(("arbitrary",)),
        name="router",
    )(x, g, wr)


def _moe_kernel(meta_ref, h_ref, x_ref, dest_ref, wt_ref, w1_ref, w3_ref, w2_ref, fg_ref, out_ref, hbuf, ybuf):
    i, e, c = pl.program_id(0), pl.program_id(1), pl.program_id(2)
    last_c = pl.num_programs(2) - 1
    mrow = i * N_EXPERTS + e
    count = meta_ref[mrow, MOE_NSUB]

    @pl.when((i == 0) & (e == 0) & (c == 0))
    def _():
        hbuf[...] = jnp.zeros(hbuf.shape, BF16)
        ybuf[...] = jnp.zeros(ybuf.shape, F32)

    @pl.when((e == 0) & (c == 0))
    def _():
        out_ref[...] = x_ref[...]

    def sub_tile(s):
        start = meta_ref[mrow, s]
        stop = meta_ref[mrow, s + 1]
        base = pl.multiple_of((start // MOE_ALIGN) * MOE_ALIGN, MOE_ALIGN)
        tok = slice(s * MOE_SUB, (s + 1) * MOE_SUB)
        rel = dest_ref[0, pl.ds(e, 1), tok].astype(jnp.int32) - base
        return start, stop, base, tok, rel

    def one_hot(rel, rows, offset):
        r = lax.broadcasted_iota(jnp.int32, (rows, MOE_SUB), 0) + offset
        return jnp.where(r == rel, 1.0, 0.0)

    @pl.when(c == 0)
    def _():
        for s in range(MOE_NSUB):
            start, stop, base, tok, rel = sub_tile(s)
            hs = h_ref[tok, :]
            new = _dot(one_hot(rel, MOE_WIN, 0).astype(BF16), hs).astype(BF16)
            keep = lax.broadcasted_iota(jnp.int32, (MOE_WIN, 1), 0) < start - base
            hbuf[pl.ds(base, MOE_WIN), :] = jnp.where(keep, hbuf[pl.ds(base, MOE_WIN), :], new)

            @pl.when(stop - base > MOE_WIN)
            def _():
                p2 = one_hot(rel, MOE_WIN2, MOE_WIN).astype(BF16)
                hbuf[pl.ds(base + MOE_WIN, MOE_WIN2), :] = _dot(p2, hs).astype(BF16)

    def block(b, carry):
        r0 = pl.multiple_of(b * MOE_BLOCK, MOE_BLOCK)
        hb = hbuf[pl.ds(r0, MOE_BLOCK), :]
        a = _dot(hb, w1_ref[0])
        u = (a * _sigmoid(a) * _dot(hb, w3_ref[0])).astype(BF16)
        y = _dot(u, w2_ref[0])
        ybuf[pl.ds(r0, MOE_BLOCK), :] = y + jnp.where(c > 0, ybuf[pl.ds(r0, MOE_BLOCK), :], 0.0)
        return carry

    lax.fori_loop(0, (count + MOE_BLOCK - 1) // MOE_BLOCK, block, 0)

    @pl.when(c == last_c)
    def _():
        for s in range(MOE_NSUB):
            start, stop, base, tok, rel = sub_tile(s)
            wrow = wt_ref[0, pl.ds(e, 1), tok]

            def back(rows, offset):
                p = one_hot(rel, rows, offset)
                wcol = jnp.sum(p * wrow, axis=1, keepdims=True)
                row_id = lax.broadcasted_iota(jnp.int32, (rows, 1), 0) + offset
                y = ybuf[pl.ds(base + offset, rows), :]
                y = jnp.where(row_id < count - base, y * wcol, 0.0).astype(BF16)
                out_ref[tok, :] += lax.dot_general(p.astype(BF16), y, (((0,), (0,)), ((), ())),
                                                   preferred_element_type=F32)

            back(MOE_WIN, 0)

            @pl.when(stop - base > MOE_WIN)
            def _():
                back(MOE_WIN2, MOE_WIN)

        @pl.when(e == N_EXPERTS - 1)
        def _():
            out_ref[...] = _rms(out_ref[...], fg_ref[...])


def _moe(x, g, wr, w1, w3, w2, final_g):
    n, d = x.shape
    t = MOE_TILE
    nt = n // t
    f = w1.shape[2]
    h, dest, wt, meta = _router(x, g, wr)
    meta = jnp.transpose(meta[:, :, :N_EXPERTS], (0, 2, 1)).reshape(nt * N_EXPERTS, META_ROWS).astype(jnp.int32)
    once = lambda shape, imap: pl.BlockSpec(shape, imap, pipeline_mode=pl.Buffered(1))
    grid_spec = pltpu.PrefetchScalarGridSpec(
        num_scalar_prefetch=1,
        grid=(nt, N_EXPERTS, f // MOE_FC),
        in_specs=[once((t, d), lambda i, e, c, m: (i, 0)),
                  once((t, d), lambda i, e, c, m: (i, 0)),
                  pl.BlockSpec((1, N_EXPERTS, t), lambda i, e, c, m: (i, 0, 0)),
                  pl.BlockSpec((1, N_EXPERTS, t), lambda i, e, c, m: (i, 0, 0)),
                  pl.BlockSpec((1, d, MOE_FC), lambda i, e, c, m: (e, 0, c)),
                  pl.BlockSpec((1, d, MOE_FC), lambda i, e, c, m: (e, 0, c)),
                  pl.BlockSpec((1, MOE_FC, d), lambda i, e, c, m: (e, c, 0)),
                  once((1, d), lambda i, e, c, m: (0, 0))],
        out_specs=pl.BlockSpec((t, d), lambda i, e, c, m: (i, 0)),
        scratch_shapes=[pltpu.VMEM((MOE_ROWS, d), BF16), pltpu.VMEM((MOE_ROWS, d), F32)],
    )
    return pl.pallas_call(
        _moe_kernel,
        grid_spec=grid_spec,
        out_shape=jax.ShapeDtypeStruct((n, d), F32),
        compiler_params=_params(("arbitrary", "arbitrary", "arbitrary")),
        name="moe",
    )(meta, h, x, dest, wt, w1, w3, w2, final_g)


def kernel(x, mem, norm_mix_g, w_in, b_in, a_conv_w, a_conv_b, a_ln_g, a_ln_b, a_w_out, b_w_out, c_conv_w, c_w_out, w_o, norm_x_g, norm_mem_g, xq_w, xk_w, xv_w, xo_w, norm_ffn_g, ffn_w1, ffn_w3, ffn_w2, moe_router, moe_w1, moe_w3, moe_w2, final_g):
    bsz, seq, d = x.shape
    depth = w_in.shape[0]
    row = lambda v: v.reshape(1, -1).astype(F32)
    bf = lambda w: w.astype(BF16)
    for layer in range(depth):
        wl, bl = w_in[layer], b_in[layer]
        cols = lambda lo, hi: (bf(wl[:, lo:hi]), row(bl[lo:hi]))
        wa, ba = cols(OFF_A, OFF_Q)
        wqkv, bqkv = cols(OFF_Q, OFF_CB)
        wc, bc = cols(OFF_CB, OFF_G)
        wg, bg = cols(OFF_G, N_IN)
        outs = _mixer_in(x, row(norm_mix_g[layer]), wa, ba, wqkv, bqkv, wc, bc, wg, bg,
                         a_conv_w[layer], row(a_conv_b[layer]), row(a_ln_g[layer]), row(a_ln_b[layer]),
                         bf(a_w_out[layer]), c_conv_w[layer], bf(c_w_out[layer]))
        qs, ks, vs = outs[0:3], outs[3:6], outs[6:9]
        part, g1 = outs[9:11]
        att = [_dil_attn(qs[i], ks[i], vs[i], dil) for i, dil in enumerate(DIL_DILATIONS)]
        km, vm = _mem_kv(mem, row(norm_mem_g[layer]), bf(xk_w[layer]), bf(xv_w[layer]))
        x = _post(x, part, g1, [a[0] for a in att], [a[1] for a in att], bf(b_w_out[layer]), bf(w_o[layer]),
                  row(norm_x_g[layer]), bf(xq_w[layer]), km, vm, bf(xo_w[layer]))
        x2 = x.reshape(bsz * seq, d)
        gf = row(norm_ffn_g[layer])
        i = layer // 2
        if layer % 2 == 0:
            x2 = _ffn(x2, gf, bf(ffn_w1[i]), bf(ffn_w3[i]), bf(ffn_w2[i]),
                      final_g=row(final_g) if layer == depth - 1 else None)
        else:
            assert layer == depth - 1, "the expert mixer applies the final norm"
            wr = jnp.zeros((d, LANES), F32).at[:, :N_EXPERTS].set(moe_router[i]).astype(BF16)
            x2 = _moe(x2, gf, wr, bf(moe_w1[i]), bf(moe_w3[i]), bf(moe_w2[i]), row(final_g))
        x = x2.reshape(bsz, seq, d)
    return x
```

```python
import functools

import jax
import jax.numpy as jnp
from jax import lax
from jax.experimental import pallas as pl
from jax.experimental.pallas import tpu as pltpu

D_MODEL = 1024
EPS = 1e-6
CONV_A_WIDTH = 512
CONV_A_KERNEL = 31
DIL_DILATIONS = (1, 4, 16)
DIL_HEADS = 4
DIL_HEAD_DIM = 64
DIL_GROUP_WIDTH = DIL_HEADS * DIL_HEAD_DIM
DIL_WIDTH = len(DIL_DILATIONS) * DIL_GROUP_WIDTH
ATT_BLOCK = 128
SC_WIDTH = 512
SC_KERNEL = 3
OFF_A = 0
OFF_Q = OFF_A + 2 * CONV_A_WIDTH
OFF_CB = OFF_Q + 3 * DIL_WIDTH
OFF_G = OFF_CB + 3 * SC_WIDTH
N_IN = OFF_G + 3 * D_MODEL
XATTN_HEADS = 4
XATTN_HEAD_DIM = D_MODEL // XATTN_HEADS
N_EXPERTS = 8
LANES = 128
NEG = -1e30

VMEM_LIMIT = 56 * 1024 * 1024
SEQ_TILE = 512

BF16 = jnp.bfloat16
F32 = jnp.float32


def _dot(a, b):
    return jnp.dot(a, b, preferred_element_type=F32)


def _dot_nt(a, b):
    return lax.dot_general(a, b, (((1,), (1,)), ((), ())), preferred_element_type=F32)


def _rms(x, g):
    ms = jnp.mean(x * x, axis=-1, keepdims=True)
    return x * lax.rsqrt(ms + EPS) * g


def _sigmoid(x):
    return 1.0 / (1.0 + jnp.exp(-x))


def _resident(shape):
    nd = len(shape)
    return pl.BlockSpec(shape, lambda *_: (0,) * nd, pipeline_mode=pl.Buffered(1))


def _params(sem):
    return pltpu.CompilerParams(dimension_semantics=sem, vmem_limit_bytes=VMEM_LIMIT)


def _mem_kv_kernel(mem_ref, g_ref, wk_ref, wv_ref, k_ref, v_ref):
    h = _rms(mem_ref[0], g_ref[...]).astype(BF16)
    k_ref[0] = _dot(h, wk_ref[...]).astype(BF16)
    v_ref[0] = _dot(h, wv_ref[...]).astype(BF16)


def _mem_kv(mem, g, wk, wv):
    b, m, d = mem.shape
    out = jax.ShapeDtypeStruct((b, m, d), BF16)
    return pl.pallas_call(
        _mem_kv_kernel,
        grid=(b,),
        in_specs=[pl.BlockSpec((1, m, d), lambda i: (i, 0, 0)), _resident((1, d)),
                  _resident((d, d)), _resident((d, d))],
        out_specs=[pl.BlockSpec((1, m, d), lambda i: (i, 0, 0))] * 2,
        out_shape=[out, out],
        compiler_params=_params(("arbitrary",)),
        name="mem_kv",
    )(mem, g, wk, wv)


A_HALO = 32
SUBLANES = 8
C_HALO = 8
CONV_ROWS = 64


def _mixer_in_kernel(x_ref, g_ref, wa_ref, ba_ref, wqkv_ref, bqkv_ref, wc_ref, bc_ref, wg_ref, bg_ref,
                     acw_ref, acb_ref, alg_ref, alb_ref, awo_ref, ccw_ref, cwo_ref,
                     *out_and_scratch):
    qkv_refs = out_and_scratch[:9]
    part_ref, g1_ref = out_and_scratch[9:11]
    sbuf, act, cbuf = out_and_scratch[11:]
    abuf = sbuf.at[0]
    t = x_ref.shape[1]

    @pl.when(pl.program_id(1) == 0)
    def _():
        abuf[0:A_HALO, :] = jnp.zeros((A_HALO, CONV_A_WIDTH), F32)
        cbuf[0:C_HALO, :] = jnp.zeros((C_HALO, SC_WIDTH), F32)

    h = _rms(x_ref[0], g_ref[...]).astype(BF16)

    for i, ref in enumerate(qkv_refs):
        c0 = i * DIL_GROUP_WIDTH
        z = _dot(h, wqkv_ref[:, c0:c0 + DIL_GROUP_WIDTH]) + bqkv_ref[:, c0:c0 + DIL_GROUP_WIDTH]
        if i < 3:
            z = z * (DIL_HEAD_DIM ** -0.5)
        ref[0] = z.astype(BF16)

    za = _dot(h, wa_ref[...]) + ba_ref[...]
    abuf[A_HALO:A_HALO + t, :] = za[:, :CONV_A_WIDTH] * _sigmoid(za[:, CONV_A_WIDTH:])

    shifted_rows = t + A_HALO - SUBLANES
    for r in range(1, SUBLANES):
        sbuf[r, 0:shifted_rows, :] = abuf[r:r + shifted_rows, :]

    def conv_chunk(i, carry):
        base = pl.multiple_of(i * CONV_ROWS, CONV_ROWS)
        acc = jnp.zeros((CONV_ROWS, CONV_A_WIDTH), F32)
        for k in range(CONV_A_KERNEL):
            q, r = divmod(A_HALO - (CONV_A_KERNEL - 1) + k, SUBLANES)
            rows = pl.ds(pl.multiple_of(base + q * SUBLANES, SUBLANES), CONV_ROWS)
            acc = acc + sbuf[r, rows, :] * acw_ref[k:k + 1, :]
        acc = acc + acb_ref[...]
        mu = jnp.mean(acc, axis=-1, keepdims=True)
        xc = acc - mu
        y = xc * lax.rsqrt(jnp.mean(xc * xc, axis=-1, keepdims=True) + EPS)
        y = y * alg_ref[...] + alb_ref[...]
        act[pl.ds(base, CONV_ROWS), :] = (y * _sigmoid(y)).astype(BF16)
        return carry

    lax.fori_loop(0, t // CONV_ROWS, conv_chunk, 0)
    abuf[0:A_HALO, :] = abuf[t:t + A_HALO, :]
    y_a = _dot(act[...], awo_ref[...])

    zc = _dot(h, wc_ref[...]) + bc_ref[...]
    cbuf[C_HALO:C_HALO + t, :] = zc[:, SC_WIDTH:2 * SC_WIDTH] * zc[:, 2 * SC_WIDTH:]
    conv_c = jnp.zeros((t, SC_WIDTH), F32)
    for k in range(SC_KERNEL):
        off = C_HALO - (SC_KERNEL - 1) + k
        conv_c = conv_c + cbuf[off:off + t, :] * ccw_ref[k:k + 1, :]
    cbuf[0:C_HALO, :] = cbuf[t:t + C_HALO, :]
    y_c = _dot((zc[:, :SC_WIDTH] * conv_c).astype(BF16), cwo_ref[...])

    zg = _dot(h, wg_ref[...]) + bg_ref[...]
    part = _sigmoid(zg[:, :D_MODEL]) * y_a + _sigmoid(zg[:, 2 * D_MODEL:]) * y_c
    part_ref[0] = part.astype(BF16)
    g1_ref[0] = _sigmoid(zg[:, D_MODEL:2 * D_MODEL]).astype(BF16)


def _mixer_in(x, g, wa, ba, wqkv, bqkv, wc, bc, wg, bg, acw, acb, alg, alb, awo, ccw, cwo):
    b, s, d = x.shape
    t = SEQ_TILE
    tile = lambda w: pl.BlockSpec((1, t, w), lambda i, j: (i, j, 0))
    weights = (g, wa, ba, wqkv, bqkv, wc, bc, wg, bg, acw, acb, alg, alb, awo, ccw, cwo)
    qkv_shape = jax.ShapeDtypeStruct((b, s, DIL_GROUP_WIDTH), BF16)
    wide = jax.ShapeDtypeStruct((b, s, d), BF16)
    return pl.pallas_call(
        _mixer_in_kernel,
        grid=(b, s // t),
        in_specs=[tile(d)] + [_resident(w.shape) for w in weights],
        out_specs=[tile(DIL_GROUP_WIDTH)] * 9 + [tile(d)] * 2,
        out_shape=[qkv_shape] * 9 + [wide] * 2,
        scratch_shapes=[pltpu.VMEM((SUBLANES, t + A_HALO, CONV_A_WIDTH), F32),
                        pltpu.VMEM((t, CONV_A_WIDTH), BF16),
                        pltpu.VMEM((t + C_HALO, SC_WIDTH), F32)],
        compiler_params=_params(("arbitrary", "arbitrary")),
        name="mixer_in",
    )(x, *weights)


ATT_SUB = 4


def _dil_attn_kernel(q_ref, kc_ref, kp_ref, vc_ref, vp_ref, o_ref, l_ref):
    first = pl.program_id(1) == 0
    qi = lax.broadcasted_iota(jnp.int32, (ATT_BLOCK, ATT_BLOCK), 0)
    kj = lax.broadcasted_iota(jnp.int32, (ATT_BLOCK, ATT_BLOCK), 1)
    bias_prev = jnp.where(kj >= qi, 0.0, NEG).astype(F32)
    bias_cur = jnp.where(kj <= qi, 0.0, NEG).astype(F32)
    lane = lax.broadcasted_iota(jnp.int32, (ATT_BLOCK, DIL_GROUP_WIDTH), 1)
    head_of_lane = lane // DIL_HEAD_DIM

    for sb in range(ATT_SUB):
        rows = slice(sb * ATT_BLOCK, (sb + 1) * ATT_BLOCK)
        q = q_ref[0, rows, :]
        if sb == 0:
            k_prev, v_prev = kp_ref[0], vp_ref[0]
            b_prev = jnp.where(first, NEG, bias_prev)
        else:
            prev_rows = slice((sb - 1) * ATT_BLOCK, sb * ATT_BLOCK)
            k_prev, v_prev = kc_ref[0, prev_rows, :], vc_ref[0, prev_rows, :]
            b_prev = bias_prev
        k_cur, v_cur = kc_ref[0, rows, :], vc_ref[0, rows, :]
        out = jnp.zeros((ATT_BLOCK, DIL_GROUP_WIDTH), F32)
        lse = jnp.zeros((ATT_BLOCK, DIL_GROUP_WIDTH), F32)
        for hd in range(DIL_HEADS):
            in_head = head_of_lane == hd
            qh = jnp.where(in_head, q, jnp.zeros_like(q))
            s_prev = _dot_nt(qh, k_prev) + b_prev
            s_cur = _dot_nt(qh, k_cur) + bias_cur
            m = jnp.maximum(jnp.max(s_prev, axis=-1, keepdims=True),
                            jnp.max(s_cur, axis=-1, keepdims=True))
            p_prev = jnp.exp(s_prev - m)
            p_cur = jnp.exp(s_cur - m)
            den = jnp.sum(p_prev, axis=-1, keepdims=True) + jnp.sum(p_cur, axis=-1, keepdims=True)
            inv = 1.0 / den
            vh_prev = jnp.where(in_head, v_prev, jnp.zeros_like(v_prev))
            vh_cur = jnp.where(in_head, v_cur, jnp.zeros_like(v_cur))
            out = out + _dot((p_prev * inv).astype(BF16), vh_prev) + _dot((p_cur * inv).astype(BF16), vh_cur)
            lse = jnp.where(in_head, m + jnp.log(den), lse)
        o_ref[0, rows, :] = out.astype(BF16)
        l_ref[0, rows, :] = lse


def _dil_attn(q, k, v, dilation):
    b, s, w = q.shape
    rows = s // dilation
    q, k, v = (a.reshape(b, rows, dilation * w) for a in (q, k, v))
    lq = ATT_SUB * ATT_BLOCK
    cur = pl.BlockSpec((1, lq, w), lambda i, n, r: (i, n, r))
    prev = pl.BlockSpec((1, ATT_BLOCK, w), lambda i, n, r: (i, jnp.maximum(n * ATT_SUB - 1, 0), r))
    o, l = pl.pallas_call(
        _dil_attn_kernel,
        grid=(b, rows // lq, dilation),
        in_specs=[cur, cur, prev, cur, prev],
        out_specs=[cur, cur],
        out_shape=[jax.ShapeDtypeStruct((b, rows, dilation * w), BF16),
                   jax.ShapeDtypeStruct((b, rows, dilation * w), F32)],
        compiler_params=_params(("arbitrary", "arbitrary", "arbitrary")),
        name=f"dil_attn_d{dilation}",
    )(q, k, k, v, v)
    return o.reshape(b, s, w), l.reshape(b, s, w)


def _post_kernel(x_ref, part_ref, g1_ref, o0_ref, o1_ref, o2_ref, l0_ref, l1_ref, l2_ref,
                 bwo_ref, wo_ref, gx_ref, xq_ref, km_ref, vm_ref, xo_ref, out_ref):
    l0, l1, l2 = l0_ref[0], l1_ref[0], l2_ref[0]
    top = jnp.maximum(jnp.maximum(l0, l1), l2)
    e0, e1, e2 = jnp.exp(l0 - top), jnp.exp(l1 - top), jnp.exp(l2 - top)
    o = (e0 * o0_ref[0].astype(F32) + e1 * o1_ref[0].astype(F32) + e2 * o2_ref[0].astype(F32)) / (e0 + e1 + e2)
    y_b = _dot(o.astype(BF16), bwo_ref[...])
    merged = part_ref[0].astype(F32) + g1_ref[0].astype(F32) * y_b
    x = x_ref[0] + _dot(merged.astype(BF16), wo_ref[...])

    h = _rms(x, gx_ref[...]).astype(BF16)
    q = (_dot(h, xq_ref[...]) * (XATTN_HEAD_DIM ** -0.5)).astype(BF16)
    for hd in range(XATTN_HEADS):
        cols = slice(hd * XATTN_HEAD_DIM, (hd + 1) * XATTN_HEAD_DIM)
        s = _dot_nt(q[:, cols], km_ref[0, :, cols])
        m = jnp.max(s, axis=-1, keepdims=True)
        p = jnp.exp(s - m)
        p = p * (1.0 / jnp.sum(p, axis=-1, keepdims=True))
        oh = _dot(p.astype(BF16), vm_ref[0, :, cols])
        x = x + _dot(oh.astype(BF16), xo_ref[cols, :])
    out_ref[0] = x


def _post(x, part, g1, outs, lses, bwo, wo, gx, xq, km, vm, xo):
    b, s, d = x.shape
    t = SEQ_TILE
    tile = lambda w: pl.BlockSpec((1, t, w), lambda i, j: (i, j, 0))
    mem_spec = pl.BlockSpec((1,) + km.shape[1:], lambda i, j: (i, 0, 0))
    weights_a = (bwo, wo, gx, xq)
    return pl.pallas_call(
        _post_kernel,
        grid=(b, s // t),
        in_specs=[tile(d)] * 3 + [tile(DIL_GROUP_WIDTH)] * 6 + [_resident(w.shape) for w in weights_a]
                 + [mem_spec, mem_spec, _resident(xo.shape)],
        out_specs=tile(d),
        out_shape=jax.ShapeDtypeStruct((b, s, d), F32),
        compiler_params=_params(("arbitrary", "arbitrary")),
        name="post_mixer",
    )(x, part, g1, *outs, *lses, *weights_a, km, vm, xo)


FFN_CHUNKS = 2


def _ffn_kernel(x_ref, g_ref, w1_ref, w3_ref, w2_ref, *rest, final):
    fg_ref = rest[0] if final else None
    out_ref = rest[-1]
    x = x_ref[...]
    h = _rms(x, g_ref[...]).astype(BF16)
    fc = w1_ref.shape[1] // FFN_CHUNKS
    y = jnp.zeros(x.shape, F32)
    for c in range(FFN_CHUNKS):
        cols = slice(c * fc, (c + 1) * fc)
        a = _dot(h, w1_ref[:, cols])
        u = (a * _sigmoid(a) * _dot(h, w3_ref[:, cols])).astype(BF16)
        y = y + _dot(u, w2_ref[cols, :])
    out = x + y
    if final:
        out = _rms(out, fg_ref[...])
    out_ref[...] = out


def _ffn(x, g, w1, w3, w2, final_g=None):
    n, d = x.shape
    t = SEQ_TILE
    row = pl.BlockSpec((t, d), lambda i: (i, 0))
    args = [x, g, w1, w3, w2] + ([final_g] if final_g is not None else [])
    return pl.pallas_call(
        functools.partial(_ffn_kernel, final=final_g is not None),
        grid=(n // t,),
        in_specs=[row] + [_resident(a.shape) for a in args[1:]],
        out_specs=row,
        out_shape=jax.ShapeDtypeStruct((n, d), F32),
        compiler_params=_params(("arbitrary",)),
        name="ffn",
    )(*args)


MOE_TILE = 2048
MOE_SUB = 256
MOE_NSUB = MOE_TILE // MOE_SUB
MOE_ALIGN = 16
MOE_WIN = 128
MOE_WIN2 = MOE_ALIGN + MOE_SUB - MOE_WIN
MOE_BLOCK = 256
MOE_FC = 896
MOE_ROWS = MOE_TILE + MOE_BLOCK
META_ROWS = 16


def _router_kernel(x_ref, g_ref, wr_ref, h_ref, dest_ref, wt_ref, meta_ref, tri_ref):
    t = x_ref.shape[0]

    @pl.when(pl.program_id(0) == 0)
    def _():
        for rb in range(0, t, MOE_SUB):
            r = lax.broadcasted_iota(jnp.int32, (MOE_SUB, t), 0) + rb
            c = lax.broadcasted_iota(jnp.int32, (MOE_SUB, t), 1)
            tri_ref[rb:rb + MOE_SUB, :] = jnp.where(c < r, 1.0, 0.0).astype(BF16)

    h = _rms(x_ref[...], g_ref[...]).astype(BF16)
    h_ref[...] = h
    logits = _dot(h, wr_ref[...])
    lane = lax.broadcasted_iota(jnp.int32, logits.shape, 1)
    lg = jnp.where(lane < N_EXPERTS, logits, -jnp.inf)
    m1 = jnp.max(lg, axis=-1, keepdims=True)
    i1 = jnp.min(jnp.where(lg == m1, lane, LANES), axis=-1, keepdims=True)
    lg2 = jnp.where(lane == i1, -jnp.inf, lg)
    m2 = jnp.max(lg2, axis=-1, keepdims=True)
    i2 = jnp.min(jnp.where(lg2 == m2, lane, LANES), axis=-1, keepdims=True)
    e = jnp.exp(m2 - m1)
    w_top = 1.0 / (1.0 + e)
    cw = jnp.where(lane == i1, w_top, 0.0) + jnp.where(lane == i2, e * w_top, 0.0)
    ind = jnp.where(lane == i1, 1.0, 0.0) + jnp.where(lane == i2, 1.0, 0.0)
    rank = _dot(tri_ref[...], ind.astype(BF16))
    dest = jnp.where(ind > 0.0, rank, -1.0)
    dest_ref[0] = dest.T[:N_EXPERTS, :]
    wt_ref[0] = cw.T[:N_EXPERTS, :]
    rows = [rank[s * MOE_SUB:s * MOE_SUB + 1, :] for s in range(MOE_NSUB)]
    rows.append(rank[t - 1:t, :] + ind[t - 1:t, :])
    rows.append(jnp.zeros((META_ROWS - len(rows), LANES), F32))
    meta_ref[0] = jnp.concatenate(rows, axis=0)


def _router(x, g, wr):
    n, d = x.shape
    t = MOE_TILE
    nt = n // t
    per_tile = lambda r, c: pl.BlockSpec((1, r, c), lambda i: (i, 0, 0))
    return pl.pallas_call(
        _router_kernel,
        grid=(nt,),
        in_specs=[pl.BlockSpec((t, d), lambda i: (i, 0)), _resident(g.shape), _resident(wr.shape)],
        out_specs=[pl.BlockSpec((t, d), lambda i: (i, 0)), per_tile(N_EXPERTS, t), per_tile(N_EXPERTS, t),
                   per_tile(META_ROWS, LANES)],
        out_shape=[jax.ShapeDtypeStruct((n, d), BF16), jax.ShapeDtypeStruct((nt, N_EXPERTS, t), F32),
                   jax.ShapeDtypeStruct((nt, N_EXPERTS, t), F32), jax.ShapeDtypeStruct((nt, META_ROWS, LANES), F32)],
        scratch_shapes=[pltpu.VMEM((t, t), BF16)],
        compiler_params=_params(("arbitrary",)),
        name="router",
    )(x, g, wr)


def _moe_kernel(meta_ref, h_ref, x_ref, dest_ref, wt_ref, w1_ref, w3_ref, w2_ref, fg_ref, out_ref, hbuf, ybuf):
    i, e, c = pl.program_id(0), pl.program_id(1), pl.program_id(2)
    last_c = pl.num_programs(2) - 1
    mrow = i * N_EXPERTS + e
    count = meta_ref[mrow, MOE_NSUB]

    @pl.when((i == 0) & (e == 0) & (c == 0))
    def _():
        hbuf[...] = jnp.zeros(hbuf.shape, BF16)
        ybuf[...] = jnp.zeros(ybuf.shape, F32)

    @pl.when((e == 0) & (c == 0))
    def _():
        out_ref[...] = x_ref[...]

    def sub_tile(s):
        start = meta_ref[mrow, s]
        stop = meta_ref[mrow, s + 1]
        base = pl.multiple_of((start // MOE_ALIGN) * MOE_ALIGN, MOE_ALIGN)
        tok = slice(s * MOE_SUB, (s + 1) * MOE_SUB)
        rel = dest_ref[0, pl.ds(e, 1), tok].astype(jnp.int32) - base
        return start, stop, base, tok, rel

    def one_hot(rel, rows, offset):
        r = lax.broadcasted_iota(jnp.int32, (rows, MOE_SUB), 0) + offset
        return jnp.where(r == rel, 1.0, 0.0)

    @pl.when(c == 0)
    def _():
        for s in range(MOE_NSUB):
            start, stop, base, tok, rel = sub_tile(s)
            hs = h_ref[tok, :]
            new = _dot(one_hot(rel, MOE_WIN, 0).astype(BF16), hs).astype(BF16)
            keep = lax.broadcasted_iota(jnp.int32, (MOE_WIN, 1), 0) < start - base
            hbuf[pl.ds(base, MOE_WIN), :] = jnp.where(keep, hbuf[pl.ds(base, MOE_WIN), :], new)

            @pl.when(stop - base > MOE_WIN)
            def _():
                p2 = one_hot(rel, MOE_WIN2, MOE_WIN).astype(BF16)
                hbuf[pl.ds(base + MOE_WIN, MOE_WIN2), :] = _dot(p2, hs).astype(BF16)

    def block(b, carry, rows=MOE_BLOCK):
        r0 = pl.multiple_of(b * MOE_BLOCK, MOE_BLOCK)
        hb = hbuf[pl.ds(r0, rows), :]
        a = _dot(hb, w1_ref[0])
        u = (a * _sigmoid(a) * _dot(hb, w3_ref[0])).astype(BF16)
        y = _dot(u, w2_ref[0])
        ybuf[pl.ds(r0, rows), :] = y + jnp.where(c > 0, ybuf[pl.ds(r0, rows), :], 0.0)
        return carry

    full_blocks = count // MOE_BLOCK
    tail = count - full_blocks * MOE_BLOCK
    lax.fori_loop(0, full_blocks, block, 0)

    @pl.when(tail > MOE_BLOCK // 2)
    def _():
        block(full_blocks, 0)

    @pl.when((tail > 0) & (tail <= MOE_BLOCK // 2))
    def _():
        block(full_blocks, 0, rows=MOE_BLOCK // 2)

    @pl.when(c == last_c)
    def _():
        for s in range(MOE_NSUB):
            start, stop, base, tok, rel = sub_tile(s)
            wrow = wt_ref[0, pl.ds(e, 1), tok]

            def back(rows, offset):
                p = one_hot(rel, rows, offset)
                wcol = jnp.sum(p * wrow, axis=1, keepdims=True)
                row_id = lax.broadcasted_iota(jnp.int32, (rows, 1), 0) + offset
                y = ybuf[pl.ds(base + offset, rows), :]
                y = jnp.where(row_id < count - base, y * wcol, 0.0).astype(BF16)
                out_ref[tok, :] += lax.dot_general(p.astype(BF16), y, (((0,), (0,)), ((), ())),
                                                   preferred_element_type=F32)

            back(MOE_WIN, 0)

            @pl.when(stop - base > MOE_WIN)
            def _():
                back(MOE_WIN2, MOE_WIN)

        @pl.when(e == N_EXPERTS - 1)
        def _():
            out_ref[...] = _rms(out_ref[...], fg_ref[...])


def _moe(x, g, wr, w1, w3, w2, final_g):
    n, d = x.shape
    t = MOE_TILE
    nt = n // t
    f = w1.shape[2]
    h, dest, wt, meta = _router(x, g, wr)
    meta = jnp.transpose(meta[:, :, :N_EXPERTS], (0, 2, 1)).reshape(nt * N_EXPERTS, META_ROWS).astype(jnp.int32)
    once = lambda shape, imap: pl.BlockSpec(shape, imap, pipeline_mode=pl.Buffered(1))
    grid_spec = pltpu.PrefetchScalarGridSpec(
        num_scalar_prefetch=1,
        grid=(nt, N_EXPERTS, f // MOE_FC),
        in_specs=[once((t, d), lambda i, e, c, m: (i, 0)),
                  once((t, d), lambda i, e, c, m: (i, 0)),
                  pl.BlockSpec((1, N_EXPERTS, t), lambda i, e, c, m: (i, 0, 0)),
                  pl.BlockSpec((1, N_EXPERTS, t), lambda i, e, c, m: (i, 0, 0)),
                  pl.BlockSpec((1, d, MOE_FC), lambda i, e, c, m: (e, 0, c)),
                  pl.BlockSpec((1, d, MOE_FC), lambda i, e, c, m: (e, 0, c)),
                  pl.BlockSpec((1, MOE_FC, d), lambda i, e, c, m: (e, c, 0)),
                  once((1, d), lambda i, e, c, m: (0, 0))],
        out_specs=pl.BlockSpec((t, d), lambda i, e, c, m: (i, 0)),
        scratch_shapes=[pltpu.VMEM((MOE_ROWS, d), BF16), pltpu.VMEM((MOE_ROWS, d), F32)],
    )
    return pl.pallas_call(
        _moe_kernel,
        grid_spec=grid_spec,
        out_shape=jax.ShapeDtypeStruct((n, d), F32),
        compiler_params=_params(("arbitrary", "arbitrary", "arbitrary")),
        name="moe",
    )(meta, h, x, dest, wt, w1, w3, w2, final_g)


def kernel(x, mem, norm_mix_g, w_in, b_in, a_conv_w, a_conv_b, a_ln_g, a_ln_b, a_w_out, b_w_out, c_conv_w, c_w_out, w_o, norm_x_g, norm_mem_g, xq_w, xk_w, xv_w, xo_w, norm_ffn_g, ffn_w1, ffn_w3, ffn_w2, moe_router, moe_w1, moe_w3, moe_w2, final_g):
    bsz, seq, d = x.shape
    depth = w_in.shape[0]
    row = lambda v: v.reshape(1, -1).astype(F32)
    bf = lambda w: w.astype(BF16)
    for layer in range(depth):
        wl, bl = w_in[layer], b_in[layer]
        cols = lambda lo, hi: (bf(wl[:, lo:hi]), row(bl[lo:hi]))
        wa, ba = cols(OFF_A, OFF_Q)
        wqkv, bqkv = cols(OFF_Q, OFF_CB)
        wc, bc = cols(OFF_CB, OFF_G)
        wg, bg = cols(OFF_G, N_IN)
        outs = _mixer_in(x, row(norm_mix_g[layer]), wa, ba, wqkv, bqkv, wc, bc, wg, bg,
                         a_conv_w[layer], row(a_conv_b[layer]), row(a_ln_g[layer]), row(a_ln_b[layer]),
                         bf(a_w_out[layer]), c_conv_w[layer], bf(c_w_out[layer]))
        qs, ks, vs = outs[0:3], outs[3:6], outs[6:9]
        part, g1 = outs[9:11]
        att = [_dil_attn(qs[i], ks[i], vs[i], dil) for i, dil in enumerate(DIL_DILATIONS)]
        km, vm = _mem_kv(mem, row(norm_mem_g[layer]), bf(xk_w[layer]), bf(xv_w[layer]))
        x = _post(x, part, g1, [a[0] for a in att], [a[1] for a in att], bf(b_w_out[layer]), bf(w_o[layer]),
                  row(norm_x_g[layer]), bf(xq_w[layer]), km, vm, bf(xo_w[layer]))
        x2 = x.reshape(bsz * seq, d)
        gf = row(norm_ffn_g[layer])
        i = layer // 2
        if layer % 2 == 0:
            x2 = _ffn(x2, gf, bf(ffn_w1[i]), bf(ffn_w3[i]), bf(ffn_w2[i]),
                      final_g=row(final_g) if layer == depth - 1 else None)
        else:
            assert layer == depth - 1, "the expert mixer applies the final norm"
            wr = jnp.zeros((d, LANES), F32).at[:, :N_EXPERTS].set(moe_router[i]).astype(BF16)
            x2 = _moe(x2, gf, wr, bf(moe_w1[i]), bf(moe_w3[i]), bf(moe_w2[i]), row(final_g))
        x = x2.reshape(bsz, seq, d)
    return x
```

```python
import functools

import jax
import jax.numpy as jnp
from jax import lax
from jax.experimental import pallas as pl
from jax.experimental.pallas import tpu as pltpu

D_MODEL = 1024
EPS = 1e-6
CONV_A_WIDTH = 512
CONV_A_KERNEL = 31
DIL_DILATIONS = (1, 4, 16)
DIL_HEADS = 4
DIL_HEAD_DIM = 64
DIL_GROUP_WIDTH = DIL_HEADS * DIL_HEAD_DIM
DIL_WIDTH = len(DIL_DILATIONS) * DIL_GROUP_WIDTH
ATT_BLOCK = 128
SC_WIDTH = 512
SC_KERNEL = 3
OFF_A = 0
OFF_Q = OFF_A + 2 * CONV_A_WIDTH
OFF_CB = OFF_Q + 3 * DIL_WIDTH
OFF_G = OFF_CB + 3 * SC_WIDTH
N_IN = OFF_G + 3 * D_MODEL
XATTN_HEADS = 4
XATTN_HEAD_DIM = D_MODEL // XATTN_HEADS
N_EXPERTS = 8
LANES = 128
NEG = -1e30

VMEM_LIMIT = 56 * 1024 * 1024
SEQ_TILE = 512

BF16 = jnp.bfloat16
F32 = jnp.float32


def _dot(a, b):
    return jnp.dot(a, b, preferred_element_type=F32)


def _dot_nt(a, b):
    return lax.dot_general(a, b, (((1,), (1,)), ((), ())), preferred_element_type=F32)


def _rms(x, g):
    ms = jnp.mean(x * x, axis=-1, keepdims=True)
    return x * lax.rsqrt(ms + EPS) * g


def _sigmoid(x):
    return 1.0 / (1.0 + jnp.exp(-x))


def _resident(shape):
    nd = len(shape)
    return pl.BlockSpec(shape, lambda *_: (0,) * nd, pipeline_mode=pl.Buffered(1))


def _params(sem):
    return pltpu.CompilerParams(dimension_semantics=sem, vmem_limit_bytes=VMEM_LIMIT)


def _mem_kv_kernel(mem_ref, g_ref, wk_ref, wv_ref, k_ref, v_ref):
    h = _rms(mem_ref[0], g_ref[...]).astype(BF16)
    k_ref[0] = _dot(h, wk_ref[...]).astype(BF16)
    v_ref[0] = _dot(h, wv_ref[...]).astype(BF16)


def _mem_kv(mem, g, wk, wv):
    b, m, d = mem.shape
    out = jax.ShapeDtypeStruct((b, m, d), BF16)
    return pl.pallas_call(
        _mem_kv_kernel,
        grid=(b,),
        in_specs=[pl.BlockSpec((1, m, d), lambda i: (i, 0, 0)), _resident((1, d)),
                  _resident((d, d)), _resident((d, d))],
        out_specs=[pl.BlockSpec((1, m, d), lambda i: (i, 0, 0))] * 2,
        out_shape=[out, out],
        compiler_params=_params(("arbitrary",)),
        name="mem_kv",
    )(mem, g, wk, wv)


A_HALO = 32
SUBLANES = 8
C_HALO = 8
CONV_ROWS = 64


def _mixer_in_kernel(x_ref, g_ref, wa_ref, ba_ref, wqkv_ref, bqkv_ref, wc_ref, bc_ref, wg_ref, bg_ref,
                     acw_ref, acb_ref, alg_ref, alb_ref, awo_ref, ccw_ref, cwo_ref,
                     *out_and_scratch):
    qkv_refs = out_and_scratch[:9]
    part_ref, g1_ref = out_and_scratch[9:11]
    sbuf, act, cbuf = out_and_scratch[11:]
    abuf = sbuf.at[0]
    t = x_ref.shape[1]

    @pl.when(pl.program_id(1) == 0)
    def _():
        abuf[0:A_HALO, :] = jnp.zeros((A_HALO, CONV_A_WIDTH), F32)
        cbuf[0:C_HALO, :] = jnp.zeros((C_HALO, SC_WIDTH), F32)

    h = _rms(x_ref[0], g_ref[...]).astype(BF16)

    for i, ref in enumerate(qkv_refs):
        c0 = i * DIL_GROUP_WIDTH
        z = _dot(h, wqkv_ref[:, c0:c0 + DIL_GROUP_WIDTH]) + bqkv_ref[:, c0:c0 + DIL_GROUP_WIDTH]
        if i < 3:
            z = z * (DIL_HEAD_DIM ** -0.5)
        ref[0] = z.astype(BF16)

    za = _dot(h, wa_ref[...]) + ba_ref[...]
    abuf[A_HALO:A_HALO + t, :] = za[:, :CONV_A_WIDTH] * _sigmoid(za[:, CONV_A_WIDTH:])

    shifted_rows = t + A_HALO - SUBLANES
    for r in range(1, SUBLANES):
        sbuf[r, 0:shifted_rows, :] = abuf[r:r + shifted_rows, :]

    def conv_chunk(i, carry):
        base = pl.multiple_of(i * CONV_ROWS, CONV_ROWS)
        acc = jnp.zeros((CONV_ROWS, CONV_A_WIDTH), F32)
        for k in range(CONV_A_KERNEL):
            q, r = divmod(A_HALO - (CONV_A_KERNEL - 1) + k, SUBLANES)
            rows = pl.ds(pl.multiple_of(base + q * SUBLANES, SUBLANES), CONV_ROWS)
            acc = acc + sbuf[r, rows, :] * acw_ref[k:k + 1, :]
        acc = acc + acb_ref[...]
        mu = jnp.mean(acc, axis=-1, keepdims=True)
        xc = acc - mu
        y = xc * lax.rsqrt(jnp.mean(xc * xc, axis=-1, keepdims=True) + EPS)
        y = y * alg_ref[...] + alb_ref[...]
        act[pl.ds(base, CONV_ROWS), :] = (y * _sigmoid(y)).astype(BF16)
        return carry

    lax.fori_loop(0, t // CONV_ROWS, conv_chunk, 0)
    abuf[0:A_HALO, :] = abuf[t:t + A_HALO, :]
    y_a = _dot(act[...], awo_ref[...])

    zc = _dot(h, wc_ref[...]) + bc_ref[...]
    cbuf[C_HALO:C_HALO + t, :] = zc[:, SC_WIDTH:2 * SC_WIDTH] * zc[:, 2 * SC_WIDTH:]
    conv_c = jnp.zeros((t, SC_WIDTH), F32)
    for k in range(SC_KERNEL):
        off = C_HALO - (SC_KERNEL - 1) + k
        conv_c = conv_c + cbuf[off:off + t, :] * ccw_ref[k:k + 1, :]
    cbuf[0:C_HALO, :] = cbuf[t:t + C_HALO, :]
    y_c = _dot((zc[:, :SC_WIDTH] * conv_c).astype(BF16), cwo_ref[...])

    zg = _dot(h, wg_ref[...]) + bg_ref[...]
    part = _sigmoid(zg[:, :D_MODEL]) * y_a + _sigmoid(zg[:, 2 * D_MODEL:]) * y_c
    part_ref[0] = part.astype(BF16)
    g1_ref[0] = _sigmoid(zg[:, D_MODEL:2 * D_MODEL]).astype(BF16)


def _mixer_in(x, g, wa, ba, wqkv, bqkv, wc, bc, wg, bg, acw, acb, alg, alb, awo, ccw, cwo):
    b, s, d = x.shape
    t = SEQ_TILE
    tile = lambda w: pl.BlockSpec((1, t, w), lambda i, j: (i, j, 0))
    weights = (g, wa, ba, wqkv, bqkv, wc, bc, wg, bg, acw, acb, alg, alb, awo, ccw, cwo)
    qkv_shape = jax.ShapeDtypeStruct((b, s, DIL_GROUP_WIDTH), BF16)
    wide = jax.ShapeDtypeStruct((b, s, d), BF16)
    return pl.pallas_call(
        _mixer_in_kernel,
        grid=(b, s // t),
        in_specs=[tile(d)] + [_resident(w.shape) for w in weights],
        out_specs=[tile(DIL_GROUP_WIDTH)] * 9 + [tile(d)] * 2,
        out_shape=[qkv_shape] * 9 + [wide] * 2,
        scratch_shapes=[pltpu.VMEM((SUBLANES, t + A_HALO, CONV_A_WIDTH), F32),
                        pltpu.VMEM((t, CONV_A_WIDTH), BF16),
                        pltpu.VMEM((t + C_HALO, SC_WIDTH), F32)],
        compiler_params=_params(("arbitrary", "arbitrary")),
        name="mixer_in",
    )(x, *weights)


ATT_SUB = 8


def _dil_attn_kernel(q_ref, kc_ref, kp_ref, vc_ref, vp_ref, o_ref, l_ref, kw_ref, vw_ref):
    first = pl.program_id(1) == 0
    win = 2 * ATT_BLOCK
    qi = lax.broadcasted_iota(jnp.int32, (ATT_BLOCK, win), 0)
    kj = lax.broadcasted_iota(jnp.int32, (ATT_BLOCK, win), 1)
    bias = jnp.where((kj >= qi) & (kj <= qi + ATT_BLOCK), 0.0, NEG).astype(F32)
    bias_no_prev = jnp.where(kj >= ATT_BLOCK, bias, NEG)
    q_head = lax.broadcasted_iota(jnp.int32, (ATT_BLOCK, DIL_GROUP_WIDTH), 1) // DIL_HEAD_DIM
    kv_head = lax.broadcasted_iota(jnp.int32, (win, DIL_GROUP_WIDTH), 1) // DIL_HEAD_DIM

    for sb in range(ATT_SUB):
        rows = slice(sb * ATT_BLOCK, (sb + 1) * ATT_BLOCK)
        q = q_ref[0, rows, :]
        if sb == 0:
            kw_ref[0:ATT_BLOCK, :] = kp_ref[0]
            kw_ref[ATT_BLOCK:win, :] = kc_ref[0, rows, :]
            vw_ref[0:ATT_BLOCK, :] = vp_ref[0]
            vw_ref[ATT_BLOCK:win, :] = vc_ref[0, rows, :]
            k_win, v_win = kw_ref[...], vw_ref[...]
            b_win = jnp.where(first, bias_no_prev, bias)
        else:
            win_rows = slice((sb - 1) * ATT_BLOCK, (sb + 1) * ATT_BLOCK)
            k_win, v_win = kc_ref[0, win_rows, :], vc_ref[0, win_rows, :]
            b_win = bias
        out = jnp.zeros((ATT_BLOCK, DIL_GROUP_WIDTH), F32)
        lse = jnp.zeros((ATT_BLOCK, DIL_GROUP_WIDTH), F32)
        for hd in range(DIL_HEADS):
            qh = jnp.where(q_head == hd, q, jnp.zeros_like(q))
            s = _dot_nt(qh, k_win) + b_win
            m = jnp.max(s, axis=-1, keepdims=True)
            p = jnp.exp(s - m)
            den = jnp.sum(p, axis=-1, keepdims=True)
            vh = jnp.where(kv_head == hd, v_win, jnp.zeros_like(v_win))
            out = out + _dot((p * (1.0 / den)).astype(BF16), vh)
            lse = jnp.where(q_head == hd, m + jnp.log(den), lse)
        o_ref[0, rows, :] = out.astype(BF16)
        l_ref[0, rows, :] = lse


def _dil_attn(q, k, v, dilation):
    b, s, w = q.shape
    rows = s // dilation
    q, k, v = (a.reshape(b, rows, dilation * w) for a in (q, k, v))
    lq = ATT_SUB * ATT_BLOCK
    cur = pl.BlockSpec((1, lq, w), lambda i, n, r: (i, n, r))
    prev = pl.BlockSpec((1, ATT_BLOCK, w), lambda i, n, r: (i, jnp.maximum(n * ATT_SUB - 1, 0), r))
    o, l = pl.pallas_call(
        _dil_attn_kernel,
        grid=(b, rows // lq, dilation),
        in_specs=[cur, cur, prev, cur, prev],
        out_specs=[cur, cur],
        out_shape=[jax.ShapeDtypeStruct((b, rows, dilation * w), BF16),
                   jax.ShapeDtypeStruct((b, rows, dilation * w), F32)],
        scratch_shapes=[pltpu.VMEM((2 * ATT_BLOCK, w), BF16)] * 2,
        compiler_params=_params(("arbitrary", "arbitrary", "arbitrary")),
        name=f"dil_attn_d{dilation}",
    )(q, k, k, v, v)
    return o.reshape(b, s, w), l.reshape(b, s, w)


def _post_kernel(x_ref, part_ref, g1_ref, o0_ref, o1_ref, o2_ref, l0_ref, l1_ref, l2_ref,
                 bwo_ref, wo_ref, gx_ref, xq_ref, km_ref, vm_ref, xo_ref, out_ref):
    l0, l1, l2 = l0_ref[0], l1_ref[0], l2_ref[0]
    top = jnp.maximum(jnp.maximum(l0, l1), l2)
    e0, e1, e2 = jnp.exp(l0 - top), jnp.exp(l1 - top), jnp.exp(l2 - top)
    o = (e0 * o0_ref[0].astype(F32) + e1 * o1_ref[0].astype(F32) + e2 * o2_ref[0].astype(F32)) / (e0 + e1 + e2)
    y_b = _dot(o.astype(BF16), bwo_ref[...])
    merged = part_ref[0].astype(F32) + g1_ref[0].astype(F32) * y_b
    x = x_ref[0] + _dot(merged.astype(BF16), wo_ref[...])

    h = _rms(x, gx_ref[...]).astype(BF16)
    q = (_dot(h, xq_ref[...]) * (XATTN_HEAD_DIM ** -0.5)).astype(BF16)
    for hd in range(XATTN_HEADS):
        cols = slice(hd * XATTN_HEAD_DIM, (hd + 1) * XATTN_HEAD_DIM)
        s = _dot_nt(q[:, cols], km_ref[0, :, cols])
        m = jnp.max(s, axis=-1, keepdims=True)
        p = jnp.exp(s - m)
        p = p * (1.0 / jnp.sum(p, axis=-1, keepdims=True))
        oh = _dot(p.astype(BF16), vm_ref[0, :, cols])
        x = x + _dot(oh.astype(BF16), xo_ref[cols, :])
    out_ref[0] = x


def _post(x, part, g1, outs, lses, bwo, wo, gx, xq, km, vm, xo):
    b, s, d = x.shape
    t = SEQ_TILE
    tile = lambda w: pl.BlockSpec((1, t, w), lambda i, j: (i, j, 0))
    mem_spec = pl.BlockSpec((1,) + km.shape[1:], lambda i, j: (i, 0, 0))
    weights_a = (bwo, wo, gx, xq)
    return pl.pallas_call(
        _post_kernel,
        grid=(b, s // t),
        in_specs=[tile(d)] * 3 + [tile(DIL_GROUP_WIDTH)] * 6 + [_resident(w.shape) for w in weights_a]
                 + [mem_spec, mem_spec, _resident(xo.shape)],
        out_specs=tile(d),
        out_shape=jax.ShapeDtypeStruct((b, s, d), F32),
        compiler_params=_params(("arbitrary", "arbitrary")),
        name="post_mixer",
    )(x, part, g1, *outs, *lses, *weights_a, km, vm, xo)


FFN_CHUNKS = 2


def _ffn_kernel(x_ref, g_ref, w1_ref, w3_ref, w2_ref, *rest, final):
    fg_ref = rest[0] if final else None
    out_ref = rest[-1]
    x = x_ref[...]
    h = _rms(x, g_ref[...]).astype(BF16)
    fc = w1_ref.shape[1] // FFN_CHUNKS
    y = jnp.zeros(x.shape, F32)
    for c in range(FFN_CHUNKS):
        cols = slice(c * fc, (c + 1) * fc)
        a = _dot(h, w1_ref[:, cols])
        u = (a * _sigmoid(a) * _dot(h, w3_ref[:, cols])).astype(BF16)
        y = y + _dot(u, w2_ref[cols, :])
    out = x + y
    if final:
        out = _rms(out, fg_ref[...])
    out_ref[...] = out


def _ffn(x, g, w1, w3, w2, final_g=None):
    n, d = x.shape
    t = SEQ_TILE
    row = pl.BlockSpec((t, d), lambda i: (i, 0))
    args = [x, g, w1, w3, w2] + ([final_g] if final_g is not None else [])
    return pl.pallas_call(
        functools.partial(_ffn_kernel, final=final_g is not None),
        grid=(n // t,),
        in_specs=[row] + [_resident(a.shape) for a in args[1:]],
        out_specs=row,
        out_shape=jax.ShapeDtypeStruct((n, d), F32),
        compiler_params=_params(("arbitrary",)),
        name="ffn",
    )(*args)


MOE_TILE = 2048
MOE_SUB = 256
MOE_NSUB = MOE_TILE // MOE_SUB
MOE_ALIGN = 16
MOE_WIN = 128
MOE_WIN2 = MOE_ALIGN + MOE_SUB - MOE_WIN
MOE_BLOCK = 256
MOE_TAIL = 64
MOE_FC = 896
MOE_ROWS = MOE_TILE + MOE_BLOCK
META_ROWS = 16


def _router_kernel(x_ref, g_ref, wr_ref, h_ref, dest_ref, wt_ref, meta_ref, tri_ref):
    t = x_ref.shape[0]

    @pl.when(pl.program_id(0) == 0)
    def _():
        for rb in range(0, t, MOE_SUB):
            r = lax.broadcasted_iota(jnp.int32, (MOE_SUB, t), 0) + rb
            c = lax.broadcasted_iota(jnp.int32, (MOE_SUB, t), 1)
            tri_ref[rb:rb + MOE_SUB, :] = jnp.where(c < r, 1.0, 0.0).astype(BF16)

    h = _rms(x_ref[...], g_ref[...]).astype(BF16)
    h_ref[...] = h
    logits = _dot(h, wr_ref[...])
    lane = lax.broadcasted_iota(jnp.int32, logits.shape, 1)
    lg = jnp.where(lane < N_EXPERTS, logits, -jnp.inf)
    m1 = jnp.max(lg, axis=-1, keepdims=True)
    i1 = jnp.min(jnp.where(lg == m1, lane, LANES), axis=-1, keepdims=True)
    lg2 = jnp.where(lane == i1, -jnp.inf, lg)
    m2 = jnp.max(lg2, axis=-1, keepdims=True)
    i2 = jnp.min(jnp.where(lg2 == m2, lane, LANES), axis=-1, keepdims=True)
    e = jnp.exp(m2 - m1)
    w_top = 1.0 / (1.0 + e)
    cw = jnp.where(lane == i1, w_top, 0.0) + jnp.where(lane == i2, e * w_top, 0.0)
    ind = jnp.where(lane == i1, 1.0, 0.0) + jnp.where(lane == i2, 1.0, 0.0)
    rank = _dot(tri_ref[...], ind.astype(BF16))
    dest = jnp.where(ind > 0.0, rank, -1.0)
    dest_ref[0] = dest.T[:N_EXPERTS, :]
    wt_ref[0] = cw.T[:N_EXPERTS, :]
    rows = [rank[s * MOE_SUB:s * MOE_SUB + 1, :] for s in range(MOE_NSUB)]
    rows.append(rank[t - 1:t, :] + ind[t - 1:t, :])
    rows.append(jnp.zeros((META_ROWS - len(rows), LANES), F32))
    meta_ref[0] = jnp.concatenate(rows, axis=0)


def _router(x, g, wr):
    n, d = x.shape
    t = MOE_TILE
    nt = n // t
    per_tile = lambda r, c: pl.BlockSpec((1, r, c), lambda i: (i, 0, 0))
    return pl.pallas_call(
        _router_kernel,
        grid=(nt,),
        in_specs=[pl.BlockSpec((t, d), lambda i: (i, 0)), _resident(g.shape), _resident(wr.shape)],
        out_specs=[pl.BlockSpec((t, d), lambda i: (i, 0)), per_tile(N_EXPERTS, t), per_tile(N_EXPERTS, t),
                   per_tile(META_ROWS, LANES)],
        out_shape=[jax.ShapeDtypeStruct((n, d), BF16), jax.ShapeDtypeStruct((nt, N_EXPERTS, t), F32),
                   jax.ShapeDtypeStruct((nt, N_EXPERTS, t), F32), jax.ShapeDtypeStruct((nt, META_ROWS, LANES), F32)],
        scratch_shapes=[pltpu.VMEM((t, t), BF16)],
        compiler_params=_params(("arbitrary",)),
        name="router",
    )(x, g, wr)


def _moe_kernel(meta_ref, h_ref, x_ref, dest_ref, wt_ref, w1_ref, w3_ref, w2_ref, fg_ref, out_ref, hbuf, ybuf):
    i, e, c = pl.program_id(0), pl.program_id(1), pl.program_id(2)
    last_c = pl.num_programs(2) - 1
    mrow = i * N_EXPERTS + e
    count = meta_ref[mrow, MOE_NSUB]

    @pl.when((i == 0) & (e == 0) & (c == 0))
    def _():
        hbuf[...] = jnp.zeros(hbuf.shape, BF16)
        ybuf[...] = jnp.zeros(ybuf.shape, F32)

    @pl.when((e == 0) & (c == 0))
    def _():
        out_ref[...] = x_ref[...]

    def sub_tile(s):
        start = meta_ref[mrow, s]
        stop = meta_ref[mrow, s + 1]
        base = pl.multiple_of((start // MOE_ALIGN) * MOE_ALIGN, MOE_ALIGN)
        tok = slice(s * MOE_SUB, (s + 1) * MOE_SUB)
        rel = dest_ref[0, pl.ds(e, 1), tok].astype(jnp.int32) - base
        return start, stop, base, tok, rel

    def one_hot(rel, rows, offset):
        r = lax.broadcasted_iota(jnp.int32, (rows, MOE_SUB), 0) + offset
        return jnp.where(r == rel, 1.0, 0.0)

    @pl.when(c == 0)
    def _():
        for s in range(MOE_NSUB):
            start, stop, base, tok, rel = sub_tile(s)
            hs = h_ref[tok, :]
            new = _dot(one_hot(rel, MOE_WIN, 0).astype(BF16), hs).astype(BF16)
            keep = lax.broadcasted_iota(jnp.int32, (MOE_WIN, 1), 0) < start - base
            hbuf[pl.ds(base, MOE_WIN), :] = jnp.where(keep, hbuf[pl.ds(base, MOE_WIN), :], new)

            @pl.when(stop - base > MOE_WIN)
            def _():
                p2 = one_hot(rel, MOE_WIN2, MOE_WIN).astype(BF16)
                hbuf[pl.ds(base + MOE_WIN, MOE_WIN2), :] = _dot(p2, hs).astype(BF16)

    def block(b, carry, rows=MOE_BLOCK):
        r0 = pl.multiple_of(b * MOE_BLOCK, MOE_BLOCK)
        hb = hbuf[pl.ds(r0, rows), :]
        a = _dot(hb, w1_ref[0])
        u = (a * _sigmoid(a) * _dot(hb, w3_ref[0])).astype(BF16)
        y = _dot(u, w2_ref[0])
        ybuf[pl.ds(r0, rows), :] = y + jnp.where(c > 0, ybuf[pl.ds(r0, rows), :], 0.0)
        return carry

    full_blocks = count // MOE_BLOCK
    tail = count - full_blocks * MOE_BLOCK
    lax.fori_loop(0, full_blocks, block, 0)

    for rows in range(MOE_TAIL, MOE_BLOCK + 1, MOE_TAIL):
        @pl.when((tail > rows - MOE_TAIL) & (tail <= rows))
        def _(rows=rows):
            block(full_blocks, 0, rows=rows)

    @pl.when(c == last_c)
    def _():
        for s in range(MOE_NSUB):
            start, stop, base, tok, rel = sub_tile(s)
            wrow = wt_ref[0, pl.ds(e, 1), tok]

            def back(rows, offset):
                p = one_hot(rel, rows, offset)
                wcol = jnp.sum(p * wrow, axis=1, keepdims=True)
                row_id = lax.broadcasted_iota(jnp.int32, (rows, 1), 0) + offset
                y = ybuf[pl.ds(base + offset, rows), :]
                y = jnp.where(row_id < count - base, y * wcol, 0.0).astype(BF16)
                out_ref[tok, :] += lax.dot_general(p.astype(BF16), y, (((0,), (0,)), ((), ())),
                                                   preferred_element_type=F32)

            back(MOE_WIN, 0)

            @pl.when(stop - base > MOE_WIN)
            def _():
                back(MOE_WIN2, MOE_WIN)

        @pl.when(e == N_EXPERTS - 1)
        def _():
            out_ref[...] = _rms(out_ref[...], fg_ref[...])


def _moe(x, g, wr, w1, w3, w2, final_g):
    n, d = x.shape
    t = MOE_TILE
    nt = n // t
    f = w1.shape[2]
    assert n % t == 0 and f % MOE_FC == 0, (n, f)
    h, dest, wt, meta = _router(x, g, wr)
    meta = jnp.transpose(meta[:, :, :N_EXPERTS], (0, 2, 1)).reshape(nt * N_EXPERTS, META_ROWS).astype(jnp.int32)
    once = lambda shape, imap: pl.BlockSpec(shape, imap, pipeline_mode=pl.Buffered(1))
    grid_spec = pltpu.PrefetchScalarGridSpec(
        num_scalar_prefetch=1,
        grid=(nt, N_EXPERTS, f // MOE_FC),
        in_specs=[once((t, d), lambda i, e, c, m: (i, 0)),
                  once((t, d), lambda i, e, c, m: (i, 0)),
                  pl.BlockSpec((1, N_EXPERTS, t), lambda i, e, c, m: (i, 0, 0)),
                  pl.BlockSpec((1, N_EXPERTS, t), lambda i, e, c, m: (i, 0, 0)),
                  pl.BlockSpec((1, d, MOE_FC), lambda i, e, c, m: (e, 0, c)),
                  pl.BlockSpec((1, d, MOE_FC), lambda i, e, c, m: (e, 0, c)),
                  pl.BlockSpec((1, MOE_FC, d), lambda i, e, c, m: (e, c, 0)),
                  once((1, d), lambda i, e, c, m: (0, 0))],
        out_specs=pl.BlockSpec((t, d), lambda i, e, c, m: (i, 0)),
        scratch_shapes=[pltpu.VMEM((MOE_ROWS, d), BF16), pltpu.VMEM((MOE_ROWS, d), F32)],
    )
    return pl.pallas_call(
        _moe_kernel,
        grid_spec=grid_spec,
        out_shape=jax.ShapeDtypeStruct((n, d), F32),
        compiler_params=_params(("arbitrary", "arbitrary", "arbitrary")),
        name="moe",
    )(meta, h, x, dest, wt, w1, w3, w2, final_g)


def kernel(x, mem, norm_mix_g, w_in, b_in, a_conv_w, a_conv_b, a_ln_g, a_ln_b, a_w_out, b_w_out, c_conv_w, c_w_out, w_o, norm_x_g, norm_mem_g, xq_w, xk_w, xv_w, xo_w, norm_ffn_g, ffn_w1, ffn_w3, ffn_w2, moe_router, moe_w1, moe_w3, moe_w2, final_g):
    bsz, seq, d = x.shape
    depth = w_in.shape[0]
    row = lambda v: v.reshape(1, -1).astype(F32)
    bf = lambda w: w.astype(BF16)
    for layer in range(depth):
        wl, bl = w_in[layer], b_in[layer]
        cols = lambda lo, hi: (bf(wl[:, lo:hi]), row(bl[lo:hi]))
        wa, ba = cols(OFF_A, OFF_Q)
        wqkv, bqkv = cols(OFF_Q, OFF_CB)
        wc, bc = cols(OFF_CB, OFF_G)
        wg, bg = cols(OFF_G, N_IN)
        outs = _mixer_in(x, row(norm_mix_g[layer]), wa, ba, wqkv, bqkv, wc, bc, wg, bg,
                         a_conv_w[layer], row(a_conv_b[layer]), row(a_ln_g[layer]), row(a_ln_b[layer]),
                         bf(a_w_out[layer]), c_conv_w[layer], bf(c_w_out[layer]))
        qs, ks, vs = outs[0:3], outs[3:6], outs[6:9]
        part, g1 = outs[9:11]
        att = [_dil_attn(qs[i], ks[i], vs[i], dil) for i, dil in enumerate(DIL_DILATIONS)]
        km, vm = _mem_kv(mem, row(norm_mem_g[layer]), bf(xk_w[layer]), bf(xv_w[layer]))
        x = _post(x, part, g1, [a[0] for a in att], [a[1] for a in att], bf(b_w_out[layer]), bf(w_o[layer]),
                  row(norm_x_g[layer]), bf(xq_w[layer]), km, vm, bf(xo_w[layer]))
        x2 = x.reshape(bsz * seq, d)
        gf = row(norm_ffn_g[layer])
        i = layer // 2
        if layer % 2 == 0:
            x2 = _ffn(x2, gf, bf(ffn_w1[i]), bf(ffn_w3[i]), bf(ffn_w2[i]),
                      final_g=row(final_g) if layer == depth - 1 else None)
        else:
            assert layer == depth - 1, "the expert mixer applies the final norm"
            wr = jnp.zeros((d, LANES), F32).at[:, :N_EXPERTS].set(moe_router[i]).astype(BF16)
            x2 = _moe(x2, gf, wr, bf(moe_w1[i]), bf(moe_w3[i]), bf(moe_w2[i]), row(final_g))
        x = x2.reshape(bsz, seq, d)
    return x
```

```python
import functools

import jax
import jax.numpy as jnp
from jax import lax
from jax.experimental import pallas as pl
from jax.experimental.pallas import tpu as pltpu

D_MODEL = 1024
EPS = 1e-6
CONV_A_WIDTH = 512
CONV_A_KERNEL = 31
DIL_DILATIONS = (1, 4, 16)
DIL_HEADS = 4
DIL_HEAD_DIM = 64
DIL_GROUP_WIDTH = DIL_HEADS * DIL_HEAD_DIM
DIL_WIDTH = len(DIL_DILATIONS) * DIL_GROUP_WIDTH
ATT_BLOCK = 128
SC_WIDTH = 512
SC_KERNEL = 3
OFF_A = 0
OFF_Q = OFF_A + 2 * CONV_A_WIDTH
OFF_CB = OFF_Q + 3 * DIL_WIDTH
OFF_G = OFF_CB + 3 * SC_WIDTH
N_IN = OFF_G + 3 * D_MODEL
XATTN_HEADS = 4
XATTN_HEAD_DIM = D_MODEL // XATTN_HEADS
N_EXPERTS = 8
LANES = 128
NEG = -1e30

VMEM_LIMIT = 56 * 1024 * 1024
SEQ_TILE = 512

BF16 = jnp.bfloat16
F32 = jnp.float32


def _dot(a, b):
    return jnp.dot(a, b, preferred_element_type=F32)


def _dot_nt(a, b):
    return lax.dot_general(a, b, (((1,), (1,)), ((), ())), preferred_element_type=F32)


def _rms(x, g):
    ms = jnp.mean(x * x, axis=-1, keepdims=True)
    return x * lax.rsqrt(ms + EPS) * g


def _sigmoid(x):
    return 1.0 / (1.0 + jnp.exp(-x))


def _resident(shape):
    nd = len(shape)
    return pl.BlockSpec(shape, lambda *_: (0,) * nd, pipeline_mode=pl.Buffered(1))


def _params(sem):
    return pltpu.CompilerParams(dimension_semantics=sem, vmem_limit_bytes=VMEM_LIMIT)


def _mem_kv_kernel(mem_ref, g_ref, wk_ref, wv_ref, k_ref, v_ref):
    h = _rms(mem_ref[0], g_ref[...]).astype(BF16)
    k_ref[0] = _dot(h, wk_ref[...]).astype(BF16)
    v_ref[0] = _dot(h, wv_ref[...]).astype(BF16)


def _mem_kv(mem, g, wk, wv):
    b, m, d = mem.shape
    out = jax.ShapeDtypeStruct((b, m, d), BF16)
    return pl.pallas_call(
        _mem_kv_kernel,
        grid=(b,),
        in_specs=[pl.BlockSpec((1, m, d), lambda i: (i, 0, 0)), _resident((1, d)),
                  _resident((d, d)), _resident((d, d))],
        out_specs=[pl.BlockSpec((1, m, d), lambda i: (i, 0, 0))] * 2,
        out_shape=[out, out],
        compiler_params=_params(("arbitrary",)),
        name="mem_kv",
    )(mem, g, wk, wv)


A_HALO = 32
SUBLANES = 8
C_HALO = 8
CONV_ROWS = 64


def _mixer_in_kernel(x_ref, g_ref, wa_ref, ba_ref, wqkv_ref, bqkv_ref, wc_ref, bc_ref, wg_ref, bg_ref,
                     acw_ref, acb_ref, alg_ref, alb_ref, awo_ref, ccw_ref, cwo_ref,
                     *out_and_scratch):
    qkv_refs = out_and_scratch[:9]
    part_ref, g1_ref = out_and_scratch[9:11]
    sbuf, act, cbuf = out_and_scratch[11:]
    abuf = sbuf.at[0]
    t = x_ref.shape[1]

    @pl.when(pl.program_id(1) == 0)
    def _():
        abuf[0:A_HALO, :] = jnp.zeros((A_HALO, CONV_A_WIDTH), F32)
        cbuf[0:C_HALO, :] = jnp.zeros((C_HALO, SC_WIDTH), F32)

    h = _rms(x_ref[0], g_ref[...]).astype(BF16)

    def qkv_projection(i):
        c0 = i * DIL_GROUP_WIDTH
        z = _dot(h, wqkv_ref[:, c0:c0 + DIL_GROUP_WIDTH]) + bqkv_ref[:, c0:c0 + DIL_GROUP_WIDTH]
        if i < 3:
            z = z * (DIL_HEAD_DIM ** -0.5)
        qkv_refs[i][0] = z.astype(BF16)

    za = _dot(h, wa_ref[...]) + ba_ref[...]
    abuf[A_HALO:A_HALO + t, :] = za[:, :CONV_A_WIDTH] * _sigmoid(za[:, CONV_A_WIDTH:])

    shifted_rows = t + A_HALO - SUBLANES
    for r in range(1, SUBLANES):
        sbuf[r, 0:shifted_rows, :] = abuf[r:r + shifted_rows, :]

    def conv_chunk(i):
        base = i * CONV_ROWS
        acc = jnp.zeros((CONV_ROWS, CONV_A_WIDTH), F32)
        for k in range(CONV_A_KERNEL):
            q, r = divmod(A_HALO - (CONV_A_KERNEL - 1) + k, SUBLANES)
            acc = acc + sbuf[r, pl.ds(base + q * SUBLANES, CONV_ROWS), :] * acw_ref[k:k + 1, :]
        acc = acc + acb_ref[...]
        mu = jnp.mean(acc, axis=-1, keepdims=True)
        xc = acc - mu
        y = xc * lax.rsqrt(jnp.mean(xc * xc, axis=-1, keepdims=True) + EPS)
        y = y * alg_ref[...] + alb_ref[...]
        act[pl.ds(base, CONV_ROWS), :] = (y * _sigmoid(y)).astype(BF16)

    n_chunks = t // CONV_ROWS
    for i in range(max(n_chunks, len(qkv_refs))):
        if i < n_chunks:
            conv_chunk(i)
        if i < len(qkv_refs):
            qkv_projection(i)
    abuf[0:A_HALO, :] = abuf[t:t + A_HALO, :]
    y_a = _dot(act[...], awo_ref[...])

    zc = _dot(h, wc_ref[...]) + bc_ref[...]
    cbuf[C_HALO:C_HALO + t, :] = zc[:, SC_WIDTH:2 * SC_WIDTH] * zc[:, 2 * SC_WIDTH:]
    conv_c = jnp.zeros((t, SC_WIDTH), F32)
    for k in range(SC_KERNEL):
        off = C_HALO - (SC_KERNEL - 1) + k
        conv_c = conv_c + cbuf[off:off + t, :] * ccw_ref[k:k + 1, :]
    cbuf[0:C_HALO, :] = cbuf[t:t + C_HALO, :]
    y_c = _dot((zc[:, :SC_WIDTH] * conv_c).astype(BF16), cwo_ref[...])

    zg = _dot(h, wg_ref[...]) + bg_ref[...]
    part = _sigmoid(zg[:, :D_MODEL]) * y_a + _sigmoid(zg[:, 2 * D_MODEL:]) * y_c
    part_ref[0] = part.astype(BF16)
    g1_ref[0] = _sigmoid(zg[:, D_MODEL:2 * D_MODEL]).astype(BF16)


def _mixer_in(x, g, wa, ba, wqkv, bqkv, wc, bc, wg, bg, acw, acb, alg, alb, awo, ccw, cwo):
    b, s, d = x.shape
    t = SEQ_TILE
    tile = lambda w: pl.BlockSpec((1, t, w), lambda i, j: (i, j, 0))
    weights = (g, wa, ba, wqkv, bqkv, wc, bc, wg, bg, acw, acb, alg, alb, awo, ccw, cwo)
    qkv_shape = jax.ShapeDtypeStruct((b, s, DIL_GROUP_WIDTH), BF16)
    wide = jax.ShapeDtypeStruct((b, s, d), BF16)
    return pl.pallas_call(
        _mixer_in_kernel,
        grid=(b, s // t),
        in_specs=[tile(d)] + [_resident(w.shape) for w in weights],
        out_specs=[tile(DIL_GROUP_WIDTH)] * 9 + [tile(d)] * 2,
        out_shape=[qkv_shape] * 9 + [wide] * 2,
        scratch_shapes=[pltpu.VMEM((SUBLANES, t + A_HALO, CONV_A_WIDTH), F32),
                        pltpu.VMEM((t, CONV_A_WIDTH), BF16),
                        pltpu.VMEM((t + C_HALO, SC_WIDTH), F32)],
        compiler_params=_params(("arbitrary", "arbitrary")),
        name="mixer_in",
    )(x, *weights)


ATT_SUB = 8


def _dil_attn_kernel(q_ref, kc_ref, kp_ref, vc_ref, vp_ref, o_ref, l_ref, kw_ref, vw_ref):
    first = pl.program_id(1) == 0
    win = 2 * ATT_BLOCK
    qi = lax.broadcasted_iota(jnp.int32, (ATT_BLOCK, win), 0)
    kj = lax.broadcasted_iota(jnp.int32, (ATT_BLOCK, win), 1)
    bias = jnp.where((kj >= qi) & (kj <= qi + ATT_BLOCK), 0.0, NEG).astype(F32)
    bias_no_prev = jnp.where(kj >= ATT_BLOCK, bias, NEG)
    q_head = lax.broadcasted_iota(jnp.int32, (ATT_BLOCK, DIL_GROUP_WIDTH), 1) // DIL_HEAD_DIM
    kv_head = lax.broadcasted_iota(jnp.int32, (win, DIL_GROUP_WIDTH), 1) // DIL_HEAD_DIM

    for sb in range(ATT_SUB):
        rows = slice(sb * ATT_BLOCK, (sb + 1) * ATT_BLOCK)
        q = q_ref[0, rows, :]
        if sb == 0:
            kw_ref[0:ATT_BLOCK, :] = kp_ref[0]
            kw_ref[ATT_BLOCK:win, :] = kc_ref[0, rows, :]
            vw_ref[0:ATT_BLOCK, :] = vp_ref[0]
            vw_ref[ATT_BLOCK:win, :] = vc_ref[0, rows, :]
            k_win, v_win = kw_ref[...], vw_ref[...]
            b_win = jnp.where(first, bias_no_prev, bias)
        else:
            win_rows = slice((sb - 1) * ATT_BLOCK, (sb + 1) * ATT_BLOCK)
            k_win, v_win = kc_ref[0, win_rows, :], vc_ref[0, win_rows, :]
            b_win = bias
        out = jnp.zeros((ATT_BLOCK, DIL_GROUP_WIDTH), F32)
        lse = jnp.zeros((ATT_BLOCK, DIL_GROUP_WIDTH), F32)
        for hd in range(DIL_HEADS):
            qh = jnp.where(q_head == hd, q, jnp.zeros_like(q))
            s = _dot_nt(qh, k_win) + b_win
            m = jnp.max(s, axis=-1, keepdims=True)
            p = jnp.exp(s - m)
            den = jnp.sum(p, axis=-1, keepdims=True)
            vh = jnp.where(kv_head == hd, v_win, jnp.zeros_like(v_win))
            out = out + _dot((p * (1.0 / den)).astype(BF16), vh)
            lse = jnp.where(q_head == hd, m + jnp.log(den), lse)
        o_ref[0, rows, :] = out.astype(BF16)
        l_ref[0, rows, :] = lse


def _dil_attn(q, k, v, dilation):
    b, s, w = q.shape
    rows = s // dilation
    q, k, v = (a.reshape(b, rows, dilation * w) for a in (q, k, v))
    lq = ATT_SUB * ATT_BLOCK
    cur = pl.BlockSpec((1, lq, w), lambda i, n, r: (i, n, r))
    prev = pl.BlockSpec((1, ATT_BLOCK, w), lambda i, n, r: (i, jnp.maximum(n * ATT_SUB - 1, 0), r))
    o, l = pl.pallas_call(
        _dil_attn_kernel,
        grid=(b, rows // lq, dilation),
        in_specs=[cur, cur, prev, cur, prev],
        out_specs=[cur, cur],
        out_shape=[jax.ShapeDtypeStruct((b, rows, dilation * w), BF16),
                   jax.ShapeDtypeStruct((b, rows, dilation * w), F32)],
        scratch_shapes=[pltpu.VMEM((2 * ATT_BLOCK, w), BF16)] * 2,
        compiler_params=_params(("arbitrary", "arbitrary", "arbitrary")),
        name=f"dil_attn_d{dilation}",
    )(q, k, k, v, v)
    return o.reshape(b, s, w), l.reshape(b, s, w)


def _post_kernel(x_ref, part_ref, g1_ref, o0_ref, o1_ref, o2_ref, l0_ref, l1_ref, l2_ref,
                 bwo_ref, wo_ref, gx_ref, xq_ref, km_ref, vm_ref, xo_ref, out_ref):
    l0, l1, l2 = l0_ref[0], l1_ref[0], l2_ref[0]
    top = jnp.maximum(jnp.maximum(l0, l1), l2)
    e0, e1, e2 = jnp.exp(l0 - top), jnp.exp(l1 - top), jnp.exp(l2 - top)
    o = (e0 * o0_ref[0].astype(F32) + e1 * o1_ref[0].astype(F32) + e2 * o2_ref[0].astype(F32)) / (e0 + e1 + e2)
    y_b = _dot(o.astype(BF16), bwo_ref[...])
    merged = part_ref[0].astype(F32) + g1_ref[0].astype(F32) * y_b
    x = x_ref[0] + _dot(merged.astype(BF16), wo_ref[...])

    h = _rms(x, gx_ref[...]).astype(BF16)
    q = (_dot(h, xq_ref[...]) * (XATTN_HEAD_DIM ** -0.5)).astype(BF16)
    for hd in range(XATTN_HEADS):
        cols = slice(hd * XATTN_HEAD_DIM, (hd + 1) * XATTN_HEAD_DIM)
        s = _dot_nt(q[:, cols], km_ref[0, :, cols])
        m = jnp.max(s, axis=-1, keepdims=True)
        p = jnp.exp(s - m)
        p = p * (1.0 / jnp.sum(p, axis=-1, keepdims=True))
        oh = _dot(p.astype(BF16), vm_ref[0, :, cols])
        x = x + _dot(oh.astype(BF16), xo_ref[cols, :])
    out_ref[0] = x


def _post(x, part, g1, outs, lses, bwo, wo, gx, xq, km, vm, xo):
    b, s, d = x.shape
    t = SEQ_TILE
    tile = lambda w: pl.BlockSpec((1, t, w), lambda i, j: (i, j, 0))
    mem_spec = pl.BlockSpec((1,) + km.shape[1:], lambda i, j: (i, 0, 0))
    weights_a = (bwo, wo, gx, xq)
    return pl.pallas_call(
        _post_kernel,
        grid=(b, s // t),
        in_specs=[tile(d)] * 3 + [tile(DIL_GROUP_WIDTH)] * 6 + [_resident(w.shape) for w in weights_a]
                 + [mem_spec, mem_spec, _resident(xo.shape)],
        out_specs=tile(d),
        out_shape=jax.ShapeDtypeStruct((b, s, d), F32),
        compiler_params=_params(("arbitrary", "arbitrary")),
        name="post_mixer",
    )(x, part, g1, *outs, *lses, *weights_a, km, vm, xo)


FFN_CHUNKS = 2


def _ffn_kernel(x_ref, g_ref, w1_ref, w3_ref, w2_ref, *rest, final):
    fg_ref = rest[0] if final else None
    out_ref = rest[-1]
    x = x_ref[...]
    h = _rms(x, g_ref[...]).astype(BF16)
    fc = w1_ref.shape[1] // FFN_CHUNKS
    y = jnp.zeros(x.shape, F32)
    for c in range(FFN_CHUNKS):
        cols = slice(c * fc, (c + 1) * fc)
        a = _dot(h, w1_ref[:, cols])
        u = (a * _sigmoid(a) * _dot(h, w3_ref[:, cols])).astype(BF16)
        y = y + _dot(u, w2_ref[cols, :])
    out = x + y
    if final:
        out = _rms(out, fg_ref[...])
    out_ref[...] = out


def _ffn(x, g, w1, w3, w2, final_g=None):
    n, d = x.shape
    t = SEQ_TILE
    row = pl.BlockSpec((t, d), lambda i: (i, 0))
    args = [x, g, w1, w3, w2] + ([final_g] if final_g is not None else [])
    return pl.pallas_call(
        functools.partial(_ffn_kernel, final=final_g is not None),
        grid=(n // t,),
        in_specs=[row] + [_resident(a.shape) for a in args[1:]],
        out_specs=row,
        out_shape=jax.ShapeDtypeStruct((n, d), F32),
        compiler_params=_params(("arbitrary",)),
        name="ffn",
    )(*args)


MOE_TILE = 2048
MOE_SUB = 256
MOE_NSUB = MOE_TILE // MOE_SUB
MOE_ALIGN = 16
MOE_WIN = 128
MOE_WIN2 = MOE_ALIGN + MOE_SUB - MOE_WIN
MOE_BLOCK = 256
MOE_TAIL = 64
MOE_FC = 896
MOE_ROWS = MOE_TILE + MOE_BLOCK
META_ROWS = 16


def _router_kernel(x_ref, g_ref, wr_ref, h_ref, dest_ref, wt_ref, meta_ref, tri_ref):
    t = x_ref.shape[0]

    @pl.when(pl.program_id(0) == 0)
    def _():
        for rb in range(0, t, MOE_SUB):
            r = lax.broadcasted_iota(jnp.int32, (MOE_SUB, t), 0) + rb
            c = lax.broadcasted_iota(jnp.int32, (MOE_SUB, t), 1)
            tri_ref[rb:rb + MOE_SUB, :] = jnp.where(c < r, 1.0, 0.0).astype(BF16)

    h = _rms(x_ref[...], g_ref[...]).astype(BF16)
    h_ref[...] = h
    logits = _dot(h, wr_ref[...])
    lane = lax.broadcasted_iota(jnp.int32, logits.shape, 1)
    lg = jnp.where(lane < N_EXPERTS, logits, -jnp.inf)
    m1 = jnp.max(lg, axis=-1, keepdims=True)
    i1 = jnp.min(jnp.where(lg == m1, lane, LANES), axis=-1, keepdims=True)
    lg2 = jnp.where(lane == i1, -jnp.inf, lg)
    m2 = jnp.max(lg2, axis=-1, keepdims=True)
    i2 = jnp.min(jnp.where(lg2 == m2, lane, LANES), axis=-1, keepdims=True)
    e = jnp.exp(m2 - m1)
    w_top = 1.0 / (1.0 + e)
    cw = jnp.where(lane == i1, w_top, 0.0) + jnp.where(lane == i2, e * w_top, 0.0)
    ind = jnp.where(lane == i1, 1.0, 0.0) + jnp.where(lane == i2, 1.0, 0.0)
    rank = _dot(tri_ref[...], ind.astype(BF16))
    dest = jnp.where(ind > 0.0, rank, -1.0)
    dest_ref[0] = dest.T[:N_EXPERTS, :]
    wt_ref[0] = cw.T[:N_EXPERTS, :]
    rows = [rank[s * MOE_SUB:s * MOE_SUB + 1, :] for s in range(MOE_NSUB)]
    rows.append(rank[t - 1:t, :] + ind[t - 1:t, :])
    rows.append(jnp.zeros((META_ROWS - len(rows), LANES), F32))
    meta_ref[0] = jnp.concatenate(rows, axis=0)


def _router(x, g, wr):
    n, d = x.shape
    t = MOE_TILE
    nt = n // t
    per_tile = lambda r, c: pl.BlockSpec((1, r, c), lambda i: (i, 0, 0))
    return pl.pallas_call(
        _router_kernel,
        grid=(nt,),
        in_specs=[pl.BlockSpec((t, d), lambda i: (i, 0)), _resident(g.shape), _resident(wr.shape)],
        out_specs=[pl.BlockSpec((t, d), lambda i: (i, 0)), per_tile(N_EXPERTS, t), per_tile(N_EXPERTS, t),
                   per_tile(META_ROWS, LANES)],
        out_shape=[jax.ShapeDtypeStruct((n, d), BF16), jax.ShapeDtypeStruct((nt, N_EXPERTS, t), F32),
                   jax.ShapeDtypeStruct((nt, N_EXPERTS, t), F32), jax.ShapeDtypeStruct((nt, META_ROWS, LANES), F32)],
        scratch_shapes=[pltpu.VMEM((t, t), BF16)],
        compiler_params=_params(("arbitrary",)),
        name="router",
    )(x, g, wr)


def _moe_kernel(meta_ref, h_ref, x_ref, dest_ref, wt_ref, w1_ref, w3_ref, w2_ref, fg_ref, out_ref, hbuf, ybuf):
    i, e, c = pl.program_id(0), pl.program_id(1), pl.program_id(2)
    last_c = pl.num_programs(2) - 1
    mrow = i * N_EXPERTS + e
    count = meta_ref[mrow, MOE_NSUB]

    @pl.when((i == 0) & (e == 0) & (c == 0))
    def _():
        hbuf[...] = jnp.zeros(hbuf.shape, BF16)
        ybuf[...] = jnp.zeros(ybuf.shape, F32)

    @pl.when((e == 0) & (c == 0))
    def _():
        out_ref[...] = x_ref[...]

    def sub_tile(s):
        start = meta_ref[mrow, s]
        stop = meta_ref[mrow, s + 1]
        base = pl.multiple_of((start // MOE_ALIGN) * MOE_ALIGN, MOE_ALIGN)
        tok = slice(s * MOE_SUB, (s + 1) * MOE_SUB)
        rel = dest_ref[0, pl.ds(e, 1), tok].astype(jnp.int32) - base
        return start, stop, base, tok, rel

    def one_hot(rel, rows, offset):
        r = lax.broadcasted_iota(jnp.int32, (rows, MOE_SUB), 0) + offset
        return jnp.where(r == rel, 1.0, 0.0)

    @pl.when(c == 0)
    def _():
        for s in range(MOE_NSUB):
            start, stop, base, tok, rel = sub_tile(s)
            hs = h_ref[tok, :]
            new = _dot(one_hot(rel, MOE_WIN, 0).astype(BF16), hs).astype(BF16)
            keep = lax.broadcasted_iota(jnp.int32, (MOE_WIN, 1), 0) < start - base
            hbuf[pl.ds(base, MOE_WIN), :] = jnp.where(keep, hbuf[pl.ds(base, MOE_WIN), :], new)

            @pl.when(stop - base > MOE_WIN)
            def _():
                p2 = one_hot(rel, MOE_WIN2, MOE_WIN).astype(BF16)
                hbuf[pl.ds(base + MOE_WIN, MOE_WIN2), :] = _dot(p2, hs).astype(BF16)

    def block(b, carry, rows=MOE_BLOCK):
        r0 = pl.multiple_of(b * MOE_BLOCK, MOE_BLOCK)
        hb = hbuf[pl.ds(r0, rows), :]
        a = _dot(hb, w1_ref[0])
        u = (a * _sigmoid(a) * _dot(hb, w3_ref[0])).astype(BF16)
        y = _dot(u, w2_ref[0])
        ybuf[pl.ds(r0, rows), :] = y + jnp.where(c > 0, ybuf[pl.ds(r0, rows), :], 0.0)
        return carry

    full_blocks = count // MOE_BLOCK
    tail = count - full_blocks * MOE_BLOCK
    lax.fori_loop(0, full_blocks, block, 0)

    for rows in range(MOE_TAIL, MOE_BLOCK + 1, MOE_TAIL):
        @pl.when((tail > rows - MOE_TAIL) & (tail <= rows))
        def _(rows=rows):
            block(full_blocks, 0, rows=rows)

    @pl.when(c == last_c)
    def _():
        for s in range(MOE_NSUB):
            start, stop, base, tok, rel = sub_tile(s)
            wrow = wt_ref[0, pl.ds(e, 1), tok]

            def back(rows, offset):
                p = one_hot(rel, rows, offset)
                wcol = jnp.sum(p * wrow, axis=1, keepdims=True)
                row_id = lax.broadcasted_iota(jnp.int32, (rows, 1), 0) + offset
                y = ybuf[pl.ds(base + offset, rows), :]
                y = jnp.where(row_id < count - base, y * wcol, 0.0).astype(BF16)
                out_ref[tok, :] += lax.dot_general(p.astype(BF16), y, (((0,), (0,)), ((), ())),
                                                   preferred_element_type=F32)

            back(MOE_WIN, 0)

            @pl.when(stop - base > MOE_WIN)
            def _():
                back(MOE_WIN2, MOE_WIN)

        @pl.when(e == N_EXPERTS - 1)
        def _():
            out_ref[...] = _rms(out_ref[...], fg_ref[...])


def _moe(x, g, wr, w1, w3, w2, final_g):
    n, d = x.shape
    t = MOE_TILE
    nt = n // t
    f = w1.shape[2]
    assert n % t == 0 and f % MOE_FC == 0, (n, f)
    h, dest, wt, meta = _router(x, g, wr)
    meta = jnp.transpose(meta[:, :, :N_EXPERTS], (0, 2, 1)).reshape(nt * N_EXPERTS, META_ROWS).astype(jnp.int32)
    once = lambda shape, imap: pl.BlockSpec(shape, imap, pipeline_mode=pl.Buffered(1))
    grid_spec = pltpu.PrefetchScalarGridSpec(
        num_scalar_prefetch=1,
        grid=(nt, N_EXPERTS, f // MOE_FC),
        in_specs=[once((t, d), lambda i, e, c, m: (i, 0)),
                  once((t, d), lambda i, e, c, m: (i, 0)),
                  pl.BlockSpec((1, N_EXPERTS, t), lambda i, e, c, m: (i, 0, 0)),
                  pl.BlockSpec((1, N_EXPERTS, t), lambda i, e, c, m: (i, 0, 0)),
                  pl.BlockSpec((1, d, MOE_FC), lambda i, e, c, m: (e, 0, c)),
                  pl.BlockSpec((1, d, MOE_FC), lambda i, e, c, m: (e, 0, c)),
                  pl.BlockSpec((1, MOE_FC, d), lambda i, e, c, m: (e, c, 0)),
                  once((1, d), lambda i, e, c, m: (0, 0))],
        out_specs=pl.BlockSpec((t, d), lambda i, e, c, m: (i, 0)),
        scratch_shapes=[pltpu.VMEM((MOE_ROWS, d), BF16), pltpu.VMEM((MOE_ROWS, d), F32)],
    )
    return pl.pallas_call(
        _moe_kernel,
        grid_spec=grid_spec,
        out_shape=jax.ShapeDtypeStruct((n, d), F32),
        compiler_params=_params(("arbitrary", "arbitrary", "arbitrary")),
        name="moe",
    )(meta, h, x, dest, wt, w1, w3, w2, final_g)


def kernel(x, mem, norm_mix_g, w_in, b_in, a_conv_w, a_conv_b, a_ln_g, a_ln_b, a_w_out, b_w_out, c_conv_w, c_w_out, w_o, norm_x_g, norm_mem_g, xq_w, xk_w, xv_w, xo_w, norm_ffn_g, ffn_w1, ffn_w3, ffn_w2, moe_router, moe_w1, moe_w3, moe_w2, final_g):
    bsz, seq, d = x.shape
    depth = w_in.shape[0]
    row = lambda v: v.reshape(1, -1).astype(F32)
    bf = lambda w: w.astype(BF16)
    for layer in range(depth):
        wl, bl = w_in[layer], b_in[layer]
        cols = lambda lo, hi: (bf(wl[:, lo:hi]), row(bl[lo:hi]))
        wa, ba = cols(OFF_A, OFF_Q)
        wqkv, bqkv = cols(OFF_Q, OFF_CB)
        wc, bc = cols(OFF_CB, OFF_G)
        wg, bg = cols(OFF_G, N_IN)
        outs = _mixer_in(x, row(norm_mix_g[layer]), wa, ba, wqkv, bqkv, wc, bc, wg, bg,
                         a_conv_w[layer], row(a_conv_b[layer]), row(a_ln_g[layer]), row(a_ln_b[layer]),
                         bf(a_w_out[layer]), c_conv_w[layer], bf(c_w_out[layer]))
        qs, ks, vs = outs[0:3], outs[3:6], outs[6:9]
        part, g1 = outs[9:11]
        att = [_dil_attn(qs[i], ks[i], vs[i], dil) for i, dil in enumerate(DIL_DILATIONS)]
        km, vm = _mem_kv(mem, row(norm_mem_g[layer]), bf(xk_w[layer]), bf(xv_w[layer]))
        x = _post(x, part, g1, [a[0] for a in att], [a[1] for a in att], bf(b_w_out[layer]), bf(w_o[layer]),
                  row(norm_x_g[layer]), bf(xq_w[layer]), km, vm, bf(xo_w[layer]))
        x2 = x.reshape(bsz * seq, d)
        gf = row(norm_ffn_g[layer])
        i = layer // 2
        if layer % 2 == 0:
            x2 = _ffn(x2, gf, bf(ffn_w1[i]), bf(ffn_w3[i]), bf(ffn_w2[i]),
                      final_g=row(final_g) if layer == depth - 1 else None)
        else:
            assert layer == depth - 1, "the expert mixer applies the final norm"
            wr = jnp.zeros((d, LANES), F32).at[:, :N_EXPERTS].set(moe_router[i]).astype(BF16)
            x2 = _moe(x2, gf, wr, bf(moe_w1[i]), bf(moe_w3[i]), bf(moe_w2[i]), row(final_g))
        x = x2.reshape(bsz, seq, d)
    return x
```

```python
import functools

import jax
import jax.numpy as jnp
from jax import lax
from jax.experimental import pallas as pl
from jax.experimental.pallas import tpu as pltpu

D_MODEL = 1024
EPS = 1e-6
CONV_A_WIDTH = 512
CONV_A_KERNEL = 31
DIL_DILATIONS = (1, 4, 16)
DIL_HEADS = 4
DIL_HEAD_DIM = 64
DIL_GROUP_WIDTH = DIL_HEADS * DIL_HEAD_DIM
DIL_WIDTH = len(DIL_DILATIONS) * DIL_GROUP_WIDTH
ATT_BLOCK = 128
SC_WIDTH = 512
SC_KERNEL = 3
OFF_A = 0
OFF_Q = OFF_A + 2 * CONV_A_WIDTH
OFF_CB = OFF_Q + 3 * DIL_WIDTH
OFF_G = OFF_CB + 3 * SC_WIDTH
N_IN = OFF_G + 3 * D_MODEL
XATTN_HEADS = 4
XATTN_HEAD_DIM = D_MODEL // XATTN_HEADS
N_EXPERTS = 8
LANES = 128
NEG = -1e30

VMEM_LIMIT = 56 * 1024 * 1024
SEQ_TILE = 512

BF16 = jnp.bfloat16
F32 = jnp.float32


def _dot(a, b):
    return jnp.dot(a, b, preferred_element_type=F32)


def _dot_nt(a, b):
    return lax.dot_general(a, b, (((1,), (1,)), ((), ())), preferred_element_type=F32)


def _rms(x, g):
    ms = jnp.mean(x * x, axis=-1, keepdims=True)
    return x * lax.rsqrt(ms + EPS) * g


def _sigmoid(x):
    return 1.0 / (1.0 + jnp.exp(-x))


def _resident(shape):
    nd = len(shape)
    return pl.BlockSpec(shape, lambda *_: (0,) * nd, pipeline_mode=pl.Buffered(1))


def _params(sem):
    return pltpu.CompilerParams(dimension_semantics=sem, vmem_limit_bytes=VMEM_LIMIT)


def _mem_kv_kernel(mem_ref, g_ref, wk_ref, wv_ref, k_ref, v_ref):
    h = _rms(mem_ref[0], g_ref[...]).astype(BF16)
    k_ref[0] = _dot(h, wk_ref[...]).astype(BF16)
    v_ref[0] = _dot(h, wv_ref[...]).astype(BF16)


def _mem_kv(mem, g, wk, wv):
    b, m, d = mem.shape
    out = jax.ShapeDtypeStruct((b, m, d), BF16)
    return pl.pallas_call(
        _mem_kv_kernel,
        grid=(b,),
        in_specs=[pl.BlockSpec((1, m, d), lambda i: (i, 0, 0)), _resident((1, d)),
                  _resident((d, d)), _resident((d, d))],
        out_specs=[pl.BlockSpec((1, m, d), lambda i: (i, 0, 0))] * 2,
        out_shape=[out, out],
        compiler_params=_params(("arbitrary",)),
        name="mem_kv",
    )(mem, g, wk, wv)


A_HALO = 32
SUBLANES = 8
C_HALO = 8
CONV_ROWS = 64


def _mixer_in_kernel(x_ref, g_ref, wa_ref, ba_ref, wqkv_ref, bqkv_ref, wc_ref, bc_ref, wg_ref, bg_ref,
                     acw_ref, acb_ref, alg_ref, alb_ref, awo_ref, ccw_ref, cwo_ref,
                     *out_and_scratch):
    qkv_refs = out_and_scratch[:9]
    part_ref, g1_ref = out_and_scratch[9:11]
    sbuf, act, cbuf = out_and_scratch[11:]
    abuf = sbuf.at[0]
    t = x_ref.shape[1]

    @pl.when(pl.program_id(1) == 0)
    def _():
        abuf[0:A_HALO, :] = jnp.zeros((A_HALO, CONV_A_WIDTH), F32)
        cbuf[0:C_HALO, :] = jnp.zeros((C_HALO, SC_WIDTH), F32)

    h = _rms(x_ref[0], g_ref[...]).astype(BF16)

    def qkv_projection(i):
        c0 = i * DIL_GROUP_WIDTH
        z = _dot(h, wqkv_ref[:, c0:c0 + DIL_GROUP_WIDTH]) + bqkv_ref[:, c0:c0 + DIL_GROUP_WIDTH]
        if i < 3:
            z = z * (DIL_HEAD_DIM ** -0.5)
        qkv_refs[i][0] = z.astype(BF16)

    za = _dot(h, wa_ref[...]) + ba_ref[...]
    abuf[A_HALO:A_HALO + t, :] = za[:, :CONV_A_WIDTH] * _sigmoid(za[:, CONV_A_WIDTH:])

    shifted_rows = t + A_HALO - SUBLANES
    for r in range(1, SUBLANES):
        sbuf[r, 0:shifted_rows, :] = abuf[r:r + shifted_rows, :]

    def conv_chunk(i):
        base = i * CONV_ROWS
        acc = jnp.zeros((CONV_ROWS, CONV_A_WIDTH), F32)
        for k in range(CONV_A_KERNEL):
            q, r = divmod(A_HALO - (CONV_A_KERNEL - 1) + k, SUBLANES)
            acc = acc + sbuf[r, pl.ds(base + q * SUBLANES, CONV_ROWS), :] * acw_ref[k:k + 1, :]
        acc = acc + acb_ref[...]
        mu = jnp.mean(acc, axis=-1, keepdims=True)
        xc = acc - mu
        y = xc * lax.rsqrt(jnp.mean(xc * xc, axis=-1, keepdims=True) + EPS)
        y = y * alg_ref[...] + alb_ref[...]
        act[pl.ds(base, CONV_ROWS), :] = (y * _sigmoid(y)).astype(BF16)

    n_chunks = t // CONV_ROWS
    for i in range(max(n_chunks, len(qkv_refs))):
        if i < n_chunks:
            conv_chunk(i)
        if i < len(qkv_refs):
            qkv_projection(i)
    abuf[0:A_HALO, :] = abuf[t:t + A_HALO, :]
    y_a = _dot(act[...], awo_ref[...])

    zc = _dot(h, wc_ref[...]) + bc_ref[...]
    cbuf[C_HALO:C_HALO + t, :] = zc[:, SC_WIDTH:2 * SC_WIDTH] * zc[:, 2 * SC_WIDTH:]
    conv_c = jnp.zeros((t, SC_WIDTH), F32)
    for k in range(SC_KERNEL):
        off = C_HALO - (SC_KERNEL - 1) + k
        conv_c = conv_c + cbuf[off:off + t, :] * ccw_ref[k:k + 1, :]
    cbuf[0:C_HALO, :] = cbuf[t:t + C_HALO, :]
    y_c = _dot((zc[:, :SC_WIDTH] * conv_c).astype(BF16), cwo_ref[...])

    zg = _dot(h, wg_ref[...]) + bg_ref[...]
    part = _sigmoid(zg[:, :D_MODEL]) * y_a + _sigmoid(zg[:, 2 * D_MODEL:]) * y_c
    part_ref[0] = part.astype(BF16)
    g1_ref[0] = _sigmoid(zg[:, D_MODEL:2 * D_MODEL]).astype(BF16)


def _mixer_in(x, g, wa, ba, wqkv, bqkv, wc, bc, wg, bg, acw, acb, alg, alb, awo, ccw, cwo):
    b, s, d = x.shape
    t = SEQ_TILE
    tile = lambda w: pl.BlockSpec((1, t, w), lambda i, j: (i, j, 0))
    weights = (g, wa, ba, wqkv, bqkv, wc, bc, wg, bg, acw, acb, alg, alb, awo, ccw, cwo)
    qkv_shape = jax.ShapeDtypeStruct((b, s, DIL_GROUP_WIDTH), BF16)
    wide = jax.ShapeDtypeStruct((b, s, d), BF16)
    return pl.pallas_call(
        _mixer_in_kernel,
        grid=(b, s // t),
        in_specs=[tile(d)] + [_resident(w.shape) for w in weights],
        out_specs=[tile(DIL_GROUP_WIDTH)] * 9 + [tile(d)] * 2,
        out_shape=[qkv_shape] * 9 + [wide] * 2,
        scratch_shapes=[pltpu.VMEM((SUBLANES, t + A_HALO, CONV_A_WIDTH), F32),
                        pltpu.VMEM((t, CONV_A_WIDTH), BF16),
                        pltpu.VMEM((t + C_HALO, SC_WIDTH), F32)],
        compiler_params=_params(("arbitrary", "arbitrary")),
        name="mixer_in",
    )(x, *weights)


ATT_SUB = 8


def _dil_attn_kernel(q_ref, kc_ref, kp_ref, vc_ref, vp_ref, o_ref, l_ref, kw_ref, vw_ref):
    first = pl.program_id(1) == 0
    win = 2 * ATT_BLOCK
    qi = lax.broadcasted_iota(jnp.int32, (ATT_BLOCK, win), 0)
    kj = lax.broadcasted_iota(jnp.int32, (ATT_BLOCK, win), 1)
    bias = jnp.where((kj >= qi) & (kj <= qi + ATT_BLOCK), 0.0, NEG).astype(F32)
    bias_no_prev = jnp.where(kj >= ATT_BLOCK, bias, NEG)
    q_head = lax.broadcasted_iota(jnp.int32, (ATT_BLOCK, DIL_GROUP_WIDTH), 1) // DIL_HEAD_DIM
    kv_head = lax.broadcasted_iota(jnp.int32, (win, DIL_GROUP_WIDTH), 1) // DIL_HEAD_DIM

    for sb in range(ATT_SUB):
        rows = slice(sb * ATT_BLOCK, (sb + 1) * ATT_BLOCK)
        q = q_ref[0, rows, :]
        if sb == 0:
            kw_ref[0:ATT_BLOCK, :] = kp_ref[0]
            kw_ref[ATT_BLOCK:win, :] = kc_ref[0, rows, :]
            vw_ref[0:ATT_BLOCK, :] = vp_ref[0]
            vw_ref[ATT_BLOCK:win, :] = vc_ref[0, rows, :]
            k_win, v_win = kw_ref[...], vw_ref[...]
            b_win = jnp.where(first, bias_no_prev, bias)
        else:
            win_rows = slice((sb - 1) * ATT_BLOCK, (sb + 1) * ATT_BLOCK)
            k_win, v_win = kc_ref[0, win_rows, :], vc_ref[0, win_rows, :]
            b_win = bias
        out = jnp.zeros((ATT_BLOCK, DIL_GROUP_WIDTH), F32)
        lse = jnp.zeros((ATT_BLOCK, DIL_GROUP_WIDTH), F32)
        for hd in range(DIL_HEADS):
            qh = jnp.where(q_head == hd, q, jnp.zeros_like(q))
            s = _dot_nt(qh, k_win) + b_win
            m = jnp.max(s, axis=-1, keepdims=True)
            p = jnp.exp(s - m)
            den = jnp.sum(p, axis=-1, keepdims=True)
            vh = jnp.where(kv_head == hd, v_win, jnp.zeros_like(v_win))
            out = out + _dot((p * (1.0 / den)).astype(BF16), vh)
            lse = jnp.where(q_head == hd, m + jnp.log(den), lse)
        o_ref[0, rows, :] = out.astype(BF16)
        l_ref[0, rows, :] = lse


def _dil_attn(q, k, v, dilation):
    b, s, w = q.shape
    rows = s // dilation
    q, k, v = (a.reshape(b, rows, dilation * w) for a in (q, k, v))
    lq = ATT_SUB * ATT_BLOCK
    cur = pl.BlockSpec((1, lq, w), lambda i, n, r: (i, n, r))
    prev = pl.BlockSpec((1, ATT_BLOCK, w), lambda i, n, r: (i, jnp.maximum(n * ATT_SUB - 1, 0), r))
    o, l = pl.pallas_call(
        _dil_attn_kernel,
        grid=(b, rows // lq, dilation),
        in_specs=[cur, cur, prev, cur, prev],
        out_specs=[cur, cur],
        out_shape=[jax.ShapeDtypeStruct((b, rows, dilation * w), BF16),
                   jax.ShapeDtypeStruct((b, rows, dilation * w), F32)],
        scratch_shapes=[pltpu.VMEM((2 * ATT_BLOCK, w), BF16)] * 2,
        compiler_params=_params(("arbitrary", "arbitrary", "arbitrary")),
        name=f"dil_attn_d{dilation}",
    )(q, k, k, v, v)
    return o.reshape(b, s, w), l.reshape(b, s, w)


def _post_kernel(x_ref, part_ref, g1_ref, o0_ref, o1_ref, o2_ref, l0_ref, l1_ref, l2_ref,
                 bwo_ref, wo_ref, gx_ref, xq_ref, km_ref, vm_ref, xo_ref, out_ref):
    l0, l1, l2 = l0_ref[0], l1_ref[0], l2_ref[0]
    top = jnp.maximum(jnp.maximum(l0, l1), l2)
    e0, e1, e2 = jnp.exp(l0 - top), jnp.exp(l1 - top), jnp.exp(l2 - top)
    o = (e0 * o0_ref[0].astype(F32) + e1 * o1_ref[0].astype(F32) + e2 * o2_ref[0].astype(F32)) / (e0 + e1 + e2)
    y_b = _dot(o.astype(BF16), bwo_ref[...])
    merged = part_ref[0].astype(F32) + g1_ref[0].astype(F32) * y_b
    x = x_ref[0] + _dot(merged.astype(BF16), wo_ref[...])

    h = _rms(x, gx_ref[...]).astype(BF16)
    q = (_dot(h, xq_ref[...]) * (XATTN_HEAD_DIM ** -0.5)).astype(BF16)
    for hd in range(XATTN_HEADS):
        cols = slice(hd * XATTN_HEAD_DIM, (hd + 1) * XATTN_HEAD_DIM)
        s = _dot_nt(q[:, cols], km_ref[0, :, cols])
        m = jnp.max(s, axis=-1, keepdims=True)
        p = jnp.exp(s - m)
        p = p * (1.0 / jnp.sum(p, axis=-1, keepdims=True))
        oh = _dot(p.astype(BF16), vm_ref[0, :, cols])
        x = x + _dot(oh.astype(BF16), xo_ref[cols, :])
    out_ref[0] = x


def _post(x, part, g1, outs, lses, bwo, wo, gx, xq, km, vm, xo):
    b, s, d = x.shape
    t = 2 * SEQ_TILE
    tile = lambda w: pl.BlockSpec((1, t, w), lambda i, j: (i, j, 0))
    mem_spec = pl.BlockSpec((1,) + km.shape[1:], lambda i, j: (i, 0, 0))
    weights_a = (bwo, wo, gx, xq)
    return pl.pallas_call(
        _post_kernel,
        grid=(b, s // t),
        in_specs=[tile(d)] * 3 + [tile(DIL_GROUP_WIDTH)] * 6 + [_resident(w.shape) for w in weights_a]
                 + [mem_spec, mem_spec, _resident(xo.shape)],
        out_specs=tile(d),
        out_shape=jax.ShapeDtypeStruct((b, s, d), F32),
        compiler_params=_params(("arbitrary", "arbitrary")),
        name="post_mixer",
    )(x, part, g1, *outs, *lses, *weights_a, km, vm, xo)


FFN_CHUNKS = 2


def _ffn_kernel(x_ref, g_ref, w1_ref, w3_ref, w2_ref, *rest, final):
    fg_ref = rest[0] if final else None
    out_ref = rest[-1]
    x = x_ref[...]
    h = _rms(x, g_ref[...]).astype(BF16)
    fc = w1_ref.shape[1] // FFN_CHUNKS
    y = jnp.zeros(x.shape, F32)
    for c in range(FFN_CHUNKS):
        cols = slice(c * fc, (c + 1) * fc)
        a = _dot(h, w1_ref[:, cols])
        u = (a * _sigmoid(a) * _dot(h, w3_ref[:, cols])).astype(BF16)
        y = y + _dot(u, w2_ref[cols, :])
    out = x + y
    if final:
        out = _rms(out, fg_ref[...])
    out_ref[...] = out


def _ffn(x, g, w1, w3, w2, final_g=None):
    n, d = x.shape
    t = SEQ_TILE
    row = pl.BlockSpec((t, d), lambda i: (i, 0))
    args = [x, g, w1, w3, w2] + ([final_g] if final_g is not None else [])
    return pl.pallas_call(
        functools.partial(_ffn_kernel, final=final_g is not None),
        grid=(n // t,),
        in_specs=[row] + [_resident(a.shape) for a in args[1:]],
        out_specs=row,
        out_shape=jax.ShapeDtypeStruct((n, d), F32),
        compiler_params=_params(("arbitrary",)),
        name="ffn",
    )(*args)


MOE_TILE = 2048
MOE_SUB = 256
MOE_NSUB = MOE_TILE // MOE_SUB
MOE_ALIGN = 16
MOE_WIN = 128
MOE_WIN2 = MOE_ALIGN + MOE_SUB - MOE_WIN
MOE_BLOCK = 256
MOE_TAIL = 64
MOE_FC = 896
MOE_ROWS = MOE_TILE + MOE_BLOCK
META_ROWS = 16


def _router_kernel(x_ref, g_ref, wr_ref, h_ref, dest_ref, wt_ref, meta_ref, tri_ref):
    t = x_ref.shape[0]

    @pl.when(pl.program_id(0) == 0)
    def _():
        for rb in range(0, t, MOE_SUB):
            r = lax.broadcasted_iota(jnp.int32, (MOE_SUB, t), 0) + rb
            c = lax.broadcasted_iota(jnp.int32, (MOE_SUB, t), 1)
            tri_ref[rb:rb + MOE_SUB, :] = jnp.where(c < r, 1.0, 0.0).astype(BF16)

    h = _rms(x_ref[...], g_ref[...]).astype(BF16)
    h_ref[...] = h
    logits = _dot(h, wr_ref[...])
    lane = lax.broadcasted_iota(jnp.int32, logits.shape, 1)
    lg = jnp.where(lane < N_EXPERTS, logits, -jnp.inf)
    m1 = jnp.max(lg, axis=-1, keepdims=True)
    i1 = jnp.min(jnp.where(lg == m1, lane, LANES), axis=-1, keepdims=True)
    lg2 = jnp.where(lane == i1, -jnp.inf, lg)
    m2 = jnp.max(lg2, axis=-1, keepdims=True)
    i2 = jnp.min(jnp.where(lg2 == m2, lane, LANES), axis=-1, keepdims=True)
    e = jnp.exp(m2 - m1)
    w_top = 1.0 / (1.0 + e)
    cw = jnp.where(lane == i1, w_top, 0.0) + jnp.where(lane == i2, e * w_top, 0.0)
    ind = jnp.where(lane == i1, 1.0, 0.0) + jnp.where(lane == i2, 1.0, 0.0)
    rank = _dot(tri_ref[...], ind.astype(BF16))
    dest = jnp.where(ind > 0.0, rank, -1.0)
    dest_ref[0] = dest.T[:N_EXPERTS, :]
    wt_ref[0] = cw.T[:N_EXPERTS, :]
    rows = [rank[s * MOE_SUB:s * MOE_SUB + 1, :] for s in range(MOE_NSUB)]
    rows.append(rank[t - 1:t, :] + ind[t - 1:t, :])
    rows.append(jnp.zeros((META_ROWS - len(rows), LANES), F32))
    meta_ref[0] = jnp.concatenate(rows, axis=0)


def _router(x, g, wr):
    n, d = x.shape
    t = MOE_TILE
    nt = n // t
    per_tile = lambda r, c: pl.BlockSpec((1, r, c), lambda i: (i, 0, 0))
    return pl.pallas_call(
        _router_kernel,
        grid=(nt,),
        in_specs=[pl.BlockSpec((t, d), lambda i: (i, 0)), _resident(g.shape), _resident(wr.shape)],
        out_specs=[pl.BlockSpec((t, d), lambda i: (i, 0)), per_tile(N_EXPERTS, t), per_tile(N_EXPERTS, t),
                   per_tile(META_ROWS, LANES)],
        out_shape=[jax.ShapeDtypeStruct((n, d), BF16), jax.ShapeDtypeStruct((nt, N_EXPERTS, t), F32),
                   jax.ShapeDtypeStruct((nt, N_EXPERTS, t), F32), jax.ShapeDtypeStruct((nt, META_ROWS, LANES), F32)],
        scratch_shapes=[pltpu.VMEM((t, t), BF16)],
        compiler_params=_params(("arbitrary",)),
        name="router",
    )(x, g, wr)


def _moe_kernel(meta_ref, h_ref, x_ref, dest_ref, wt_ref, w1_ref, w3_ref, w2_ref, fg_ref, out_ref, hbuf, ybuf):
    i, e, c = pl.program_id(0), pl.program_id(1), pl.program_id(2)
    last_c = pl.num_programs(2) - 1
    mrow = i * N_EXPERTS + e
    count = meta_ref[mrow, MOE_NSUB]

    @pl.when((i == 0) & (e == 0) & (c == 0))
    def _():
        hbuf[...] = jnp.zeros(hbuf.shape, BF16)
        ybuf[...] = jnp.zeros(ybuf.shape, F32)

    @pl.when((e == 0) & (c == 0))
    def _():
        out_ref[...] = x_ref[...]

    def sub_tile(s):
        start = meta_ref[mrow, s]
        stop = meta_ref[mrow, s + 1]
        base = pl.multiple_of((start // MOE_ALIGN) * MOE_ALIGN, MOE_ALIGN)
        tok = slice(s * MOE_SUB, (s + 1) * MOE_SUB)
        rel = dest_ref[0, pl.ds(e, 1), tok].astype(jnp.int32) - base
        return start, stop, base, tok, rel

    def one_hot(rel, rows, offset):
        r = lax.broadcasted_iota(jnp.int32, (rows, MOE_SUB), 0) + offset
        return jnp.where(r == rel, 1.0, 0.0)

    @pl.when(c == 0)
    def _():
        for s in range(MOE_NSUB):
            start, stop, base, tok, rel = sub_tile(s)
            hs = h_ref[tok, :]
            new = _dot(one_hot(rel, MOE_WIN, 0).astype(BF16), hs).astype(BF16)
            keep = lax.broadcasted_iota(jnp.int32, (MOE_WIN, 1), 0) < start - base
            hbuf[pl.ds(base, MOE_WIN), :] = jnp.where(keep, hbuf[pl.ds(base, MOE_WIN), :], new)

            @pl.when(stop - base > MOE_WIN)
            def _():
                p2 = one_hot(rel, MOE_WIN2, MOE_WIN).astype(BF16)
                hbuf[pl.ds(base + MOE_WIN, MOE_WIN2), :] = _dot(p2, hs).astype(BF16)

    def block(b, carry, rows=MOE_BLOCK):
        r0 = pl.multiple_of(b * MOE_BLOCK, MOE_BLOCK)
        hb = hbuf[pl.ds(r0, rows), :]
        a = _dot(hb, w1_ref[0])
        u = (a * _sigmoid(a) * _dot(hb, w3_ref[0])).astype(BF16)
        y = _dot(u, w2_ref[0])
        ybuf[pl.ds(r0, rows), :] = y + jnp.where(c > 0, ybuf[pl.ds(r0, rows), :], 0.0)
        return carry

    full_blocks = count // MOE_BLOCK
    tail = count - full_blocks * MOE_BLOCK
    lax.fori_loop(0, full_blocks, block, 0)

    for rows in range(MOE_TAIL, MOE_BLOCK + 1, MOE_TAIL):
        @pl.when((tail > rows - MOE_TAIL) & (tail <= rows))
        def _(rows=rows):
            block(full_blocks, 0, rows=rows)

    @pl.when(c == last_c)
    def _():
        for s in range(MOE_NSUB):
            start, stop, base, tok, rel = sub_tile(s)
            wrow = wt_ref[0, pl.ds(e, 1), tok]

            def back(rows, offset):
                p = one_hot(rel, rows, offset)
                wcol = jnp.sum(p * wrow, axis=1, keepdims=True)
                row_id = lax.broadcasted_iota(jnp.int32, (rows, 1), 0) + offset
                y = ybuf[pl.ds(base + offset, rows), :]
                y = jnp.where(row_id < count - base, y * wcol, 0.0).astype(BF16)
                out_ref[tok, :] += lax.dot_general(p.astype(BF16), y, (((0,), (0,)), ((), ())),
                                                   preferred_element_type=F32)

            back(MOE_WIN, 0)

            @pl.when(stop - base > MOE_WIN)
            def _():
                back(MOE_WIN2, MOE_WIN)

        @pl.when(e == N_EXPERTS - 1)
        def _():
            out_ref[...] = _rms(out_ref[...], fg_ref[...])


def _moe(x, g, wr, w1, w3, w2, final_g):
    n, d = x.shape
    t = MOE_TILE
    nt = n // t
    f = w1.shape[2]
    assert n % t == 0 and f % MOE_FC == 0, (n, f)
    h, dest, wt, meta = _router(x, g, wr)
    meta = jnp.transpose(meta[:, :, :N_EXPERTS], (0, 2, 1)).reshape(nt * N_EXPERTS, META_ROWS).astype(jnp.int32)
    once = lambda shape, imap: pl.BlockSpec(shape, imap, pipeline_mode=pl.Buffered(1))
    grid_spec = pltpu.PrefetchScalarGridSpec(
        num_scalar_prefetch=1,
        grid=(nt, N_EXPERTS, f // MOE_FC),
        in_specs=[once((t, d), lambda i, e, c, m: (i, 0)),
                  once((t, d), lambda i, e, c, m: (i, 0)),
                  pl.BlockSpec((1, N_EXPERTS, t), lambda i, e, c, m: (i, 0, 0)),
                  pl.BlockSpec((1, N_EXPERTS, t), lambda i, e, c, m: (i, 0, 0)),
                  pl.BlockSpec((1, d, MOE_FC), lambda i, e, c, m: (e, 0, c)),
                  pl.BlockSpec((1, d, MOE_FC), lambda i, e, c, m: (e, 0, c)),
                  pl.BlockSpec((1, MOE_FC, d), lambda i, e, c, m: (e, c, 0)),
                  once((1, d), lambda i, e, c, m: (0, 0))],
        out_specs=pl.BlockSpec((t, d), lambda i, e, c, m: (i, 0)),
        scratch_shapes=[pltpu.VMEM((MOE_ROWS, d), BF16), pltpu.VMEM((MOE_ROWS, d), F32)],
    )
    return pl.pallas_call(
        _moe_kernel,
        grid_spec=grid_spec,
        out_shape=jax.ShapeDtypeStruct((n, d), F32),
        compiler_params=_params(("arbitrary", "arbitrary", "arbitrary")),
        name="moe",
    )(meta, h, x, dest, wt, w1, w3, w2, final_g)


def kernel(x, mem, norm_mix_g, w_in, b_in, a_conv_w, a_conv_b, a_ln_g, a_ln_b, a_w_out, b_w_out, c_conv_w, c_w_out, w_o, norm_x_g, norm_mem_g, xq_w, xk_w, xv_w, xo_w, norm_ffn_g, ffn_w1, ffn_w3, ffn_w2, moe_router, moe_w1, moe_w3, moe_w2, final_g):
    bsz, seq, d = x.shape
    depth = w_in.shape[0]
    row = lambda v: v.reshape(1, -1).astype(F32)
    bf = lambda w: w.astype(BF16)
    for layer in range(depth):
        wl, bl = w_in[layer], b_in[layer]
        cols = lambda lo, hi: (bf(wl[:, lo:hi]), row(bl[lo:hi]))
        wa, ba = cols(OFF_A, OFF_Q)
        wqkv, bqkv = cols(OFF_Q, OFF_CB)
        wc, bc = cols(OFF_CB, OFF_G)
        wg, bg = cols(OFF_G, N_IN)
        outs = _mixer_in(x, row(norm_mix_g[layer]), wa, ba, wqkv, bqkv, wc, bc, wg, bg,
                         a_conv_w[layer], row(a_conv_b[layer]), row(a_ln_g[layer]), row(a_ln_b[layer]),
                         bf(a_w_out[layer]), c_conv_w[layer], bf(c_w_out[layer]))
        qs, ks, vs = outs[0:3], outs[3:6], outs[6:9]
        part, g1 = outs[9:11]
        att = [_dil_attn(qs[i], ks[i], vs[i], dil) for i, dil in enumerate(DIL_DILATIONS)]
        km, vm = _mem_kv(mem, row(norm_mem_g[layer]), bf(xk_w[layer]), bf(xv_w[layer]))
        x = _post(x, part, g1, [a[0] for a in att], [a[1] for a in att], bf(b_w_out[layer]), bf(w_o[layer]),
                  row(norm_x_g[layer]), bf(xq_w[layer]), km, vm, bf(xo_w[layer]))
        x2 = x.reshape(bsz * seq, d)
        gf = row(norm_ffn_g[layer])
        i = layer // 2
        if layer % 2 == 0:
            x2 = _ffn(x2, gf, bf(ffn_w1[i]), bf(ffn_w3[i]), bf(ffn_w2[i]),
                      final_g=row(final_g) if layer == depth - 1 else None)
        else:
            assert layer == depth - 1, "the expert mixer applies the final norm"
            wr = jnp.zeros((d, LANES), F32).at[:, :N_EXPERTS].set(moe_router[i]).astype(BF16)
            x2 = _moe(x2, gf, wr, bf(moe_w1[i]), bf(moe_w3[i]), bf(moe_w2[i]), row(final_g))
        x = x2.reshape(bsz, seq, d)
    return x
```
